```python
import math
import jax, jax.numpy as jnp
from jax import lax
import numpy as np

D_MODEL = 1024
BATCH = 4
SEQ = 4096
DEPTH = 4

N_MIXERS = 3
RMS_EPS = 1e-6

DN_HEADS = 8
DN_DK = 128
DN_DV = 256
DN_QK_W = DN_HEADS * DN_DK
DN_V_W = DN_HEADS * DN_DV
DN_CONV = 4
DN_CHUNK = 64
DN_CONV_W = 2 * DN_QK_W + DN_V_W
DN_IN = DN_CONV_W + DN_V_W + 2 * DN_HEADS

SB_HEADS = 16
SB_DH = 64
SB_W = SB_HEADS * SB_DH
SB_BLOCK = 128
SB_IN = 4 * SB_W

SC_W = 2 * D_MODEL
SC_CONV = 3
SC_IN = 4 * SC_W

N_DN = (DEPTH + 2) // 3
N_SB = (DEPTH + 1) // 3
N_SC = DEPTH // 3

kernel_name = "interleaved_deltanet_stickbreak_shortconv"


def rms_norm(x, g, eps=RMS_EPS):
    xf = x.astype(jnp.float32)
    y = xf * lax.rsqrt(jnp.mean(xf * xf, axis=-1, keepdims=True) + eps)
    return (y * g.astype(jnp.float32)).astype(x.dtype)


def l2_norm(x, eps=1e-6):
    xf = x.astype(jnp.float32)
    return xf * lax.rsqrt(jnp.sum(xf * xf, axis=-1, keepdims=True) + eps)


def causal_dwconv(x, w):
    K, C = w.shape
    return lax.conv_general_dilated(
        x, w[:, None, :].astype(x.dtype), window_strides=(1,), padding=[(K - 1, 0)],
        dimension_numbers=('NWC', 'WIO', 'NWC'), feature_group_count=C)


def gated_delta_rule(q, k, v, log_a, beta):
    f32 = jnp.float32
    Bn, T, H, dk = q.shape
    dv = v.shape[-1]
    C = DN_CHUNK
    N = T // C

    def chunks(t):
        t = t.astype(f32).reshape((Bn, N, C, H) + t.shape[3:])
        return jnp.moveaxis(t, 3, 1)

    q = chunks(q) * (dk ** -0.5)
    k = chunks(k)
    v = chunks(v)
    beta = chunks(beta)
    g = jnp.cumsum(chunks(log_a), axis=-1)
    causal = jnp.tril(jnp.ones((C, C), bool))
    strict = jnp.tril(jnp.ones((C, C), bool), -1)
    gdiff = g[..., :, None] - g[..., None, :]
    decay = jnp.where(causal, jnp.exp(jnp.where(causal, gdiff, 0.0)), 0.0)

    k_beta = k * beta[..., None]
    L = jnp.where(strict, jnp.einsum('bhncd,bhnsd->bhncs', k_beta, k) * decay, 0.0)
    eye = jnp.eye(C, dtype=f32)
    rhs = jnp.concatenate([v * beta[..., None], k_beta * jnp.exp(g)[..., None]], axis=-1)
    sol = lax.linalg.triangular_solve(L + eye, rhs, left_side=True, lower=True, unit_diagonal=True)
    u, w = sol[..., :dv], sol[..., dv:]

    intra = jnp.where(causal, jnp.einsum('bhncd,bhnsd->bhncs', q, k) * decay, 0.0)
    q_dec = q * jnp.exp(g)[..., None]
    g_last = g[..., -1]
    k_dec = k * jnp.exp(g_last[..., None] - g)[..., None]

    def step(S, xs):
        q_c, k_c, u_c, w_c, a_c, gl = xs
        v_new = u_c - jnp.einsum('bhcd,bhde->bhce', w_c, S)
        o = jnp.einsum('bhcd,bhde->bhce', q_c, S) + jnp.einsum('bhcs,bhse->bhce', a_c, v_new)
        S = S * jnp.exp(gl)[..., None, None] + jnp.einsum('bhcd,bhce->bhde', k_c, v_new)
        return S, o

    xs = tuple(jnp.moveaxis(t, 2, 0) for t in (q_dec, k_dec, u, w, intra, g_last))
    S0 = jnp.zeros((Bn, H, dk, dv), f32)
    _, o = lax.scan(step, S0, xs)
    return jnp.transpose(o, (1, 0, 3, 2, 4)).reshape(Bn, T, H, dv)


def deltanet_mixer(h, w_in, conv_w, a_log, dt_bias, o_norm_g, w_out):
    f32 = jnp.float32
    Bn, T, _ = h.shape
    proj = h @ w_in
    qkv, gate, a_in, b_in = jnp.split(proj, [DN_CONV_W, DN_CONV_W + DN_V_W, DN_CONV_W + DN_V_W + DN_HEADS], axis=-1)
    qkv = jax.nn.silu(causal_dwconv(qkv, conv_w))
    q, k, v = jnp.split(qkv, [DN_QK_W, 2 * DN_QK_W], axis=-1)
    q = l2_norm(q.reshape(Bn, T, DN_HEADS, DN_DK))
    k = l2_norm(k.reshape(Bn, T, DN_HEADS, DN_DK))
    v = v.reshape(Bn, T, DN_HEADS, DN_DV)
    beta = jax.nn.sigmoid(b_in.astype(f32))
    log_a = -jnp.exp(a_log.astype(f32)) * jax.nn.softplus(a_in.astype(f32) + dt_bias.astype(f32))
    o = gated_delta_rule(q, k, v, log_a, beta)
    o = rms_norm(o, o_norm_g) * jax.nn.silu(gate.astype(f32).reshape(Bn, T, DN_HEADS, DN_DV))
    return o.reshape(Bn, T, DN_V_W).astype(h.dtype) @ w_out


def stick_breaking_mixer(h, w_in, q_norm_g, k_norm_g, w_out):
    f32 = jnp.float32
    Bn, T, _ = h.shape
    q, k, v, gate = jnp.split(h @ w_in, 4, axis=-1)
    q = rms_norm(q.reshape(Bn, T, SB_HEADS, SB_DH), q_norm_g)
    k = rms_norm(k.reshape(Bn, T, SB_HEADS, SB_DH), k_norm_g)
    v = v.reshape(Bn, T, SB_HEADS, SB_DH)
    nb = T // SB_BLOCK
    qb = jnp.moveaxis(q.reshape(Bn, nb, SB_BLOCK, SB_HEADS, SB_DH), 1, 0)
    key_pos = jnp.arange(T)
    scale = SB_DH ** -0.5

    def block(args):
        i, q_blk = args
        q_pos = i * SB_BLOCK + jnp.arange(SB_BLOCK)
        z = jnp.einsum('bqhd,bshd->bhqs', q_blk, k, preferred_element_type=f32) * scale
        mask = key_pos[None, :] < q_pos[:, None]
        log1m = jnp.where(mask, -jax.nn.softplus(z), 0.0)
        after = lax.cumsum(log1m, axis=3, reverse=True) - log1m
        wts = jnp.where(mask, jnp.exp(jax.nn.log_sigmoid(z) + after), 0.0)
        return jnp.einsum('bhqs,bshd->bqhd', wts.astype(v.dtype), v)

    o = lax.map(block, (jnp.arange(nb), qb))
    o = jnp.moveaxis(o, 0, 1).reshape(Bn, T, SB_W)
    return (o * jax.nn.silu(gate)) @ w_out


def short_conv_mixer(h, w_in, conv_w, w_out):
    b_gate, c_gate, u, gate = jnp.split(h @ w_in, 4, axis=-1)
    y = b_gate * causal_dwconv(c_gate * u, conv_w)
    return (y * jax.nn.silu(gate)) @ w_out


def setup_inputs(seed: int = 0) -> dict:
    key = jax.random.key(seed)
    ks = jax.random.split(key, 16)
    f32 = jnp.float32

    def dense(k, shape, fan_in):
        return jax.random.normal(k, shape, f32) * fan_in ** -0.5

    def gain(k, shape):
        return 1.0 + 0.05 * jax.random.normal(k, shape, f32)

    x = jax.random.normal(ks[0], (BATCH, SEQ, D_MODEL), f32)
    norm_g = gain(ks[1], (DEPTH, D_MODEL))
    dn_w_in = dense(ks[2], (N_DN, D_MODEL, DN_IN), D_MODEL)
    dn_conv_w = dense(ks[3], (N_DN, DN_CONV, DN_CONV_W), DN_CONV)
    dn_a_log = jnp.log(jax.random.uniform(ks[4], (N_DN, DN_HEADS), f32, 1.0, 16.0))
    dt = jnp.exp(jax.random.uniform(ks[5], (N_DN, DN_HEADS), f32, math.log(1e-3), math.log(1e-1)))
    dn_dt_bias = dt + jnp.log(-jnp.expm1(-dt))
    dn_o_norm_g = gain(ks[6], (N_DN, DN_DV))
    dn_w_out = dense(ks[7], (N_DN, DN_V_W, D_MODEL), DN_V_W)
    sb_w_in = dense(ks[8], (N_SB, D_MODEL, SB_IN), D_MODEL)
    sb_q_norm_g = gain(ks[9], (N_SB, SB_DH))
    sb_k_norm_g = gain(ks[10], (N_SB, SB_DH))
    sb_w_out = dense(ks[11], (N_SB, SB_W, D_MODEL), SB_W)
    sc_w_in = dense(ks[12], (N_SC, D_MODEL, SC_IN), D_MODEL)
    sc_conv_w = dense(ks[13], (N_SC, SC_CONV, SC_W), SC_CONV)
    sc_w_out = dense(ks[14], (N_SC, SC_W, D_MODEL), SC_W)
    return {"x": x, "norm_g": norm_g,
            "dn_w_in": dn_w_in, "dn_conv_w": dn_conv_w, "dn_a_log": dn_a_log, "dn_dt_bias": dn_dt_bias,
            "dn_o_norm_g": dn_o_norm_g, "dn_w_out": dn_w_out,
            "sb_w_in": sb_w_in, "sb_q_norm_g": sb_q_norm_g, "sb_k_norm_g": sb_k_norm_g, "sb_w_out": sb_w_out,
            "sc_w_in": sc_w_in, "sc_conv_w": sc_conv_w, "sc_w_out": sc_w_out}


def reference(x, norm_g, dn_w_in, dn_conv_w, dn_a_log, dn_dt_bias, dn_o_norm_g, dn_w_out,
              sb_w_in, sb_q_norm_g, sb_k_norm_g, sb_w_out, sc_w_in, sc_conv_w, sc_w_out):
    for i in range(DEPTH):
        h = rms_norm(x, norm_g[i])
        j = i // N_MIXERS
        kind = i % N_MIXERS
        if kind == 0:
            y = deltanet_mixer(h, dn_w_in[j], dn_conv_w[j], dn_a_log[j], dn_dt_bias[j], dn_o_norm_g[j], dn_w_out[j])
        elif kind == 1:
            y = stick_breaking_mixer(h, sb_w_in[j], sb_q_norm_g[j], sb_k_norm_g[j], sb_w_out[j])
        else:
            y = short_conv_mixer(h, sc_w_in[j], sc_conv_w[j], sc_w_out[j])
        x = x + y
    return x
```

```python
import functools

import jax
import jax.numpy as jnp
from jax import lax
from jax.experimental import pallas as pl
from jax.experimental.pallas import tpu as pltpu

F32 = jnp.float32
BF16 = jnp.bfloat16
HIGHEST = lax.Precision.HIGHEST

RMS_EPS = 1e-6
L2_EPS = 1e-6
DN_HEADS = 8
DN_DK = 128
DN_DV = 256
DN_CHUNK = 64
SB_DH = 64
LANES = 128
SUBLANES = 8
VMEM_LIMIT = 56 * 1024 * 1024

_NT = (((1,), (1,)), ((), ()))
_TN = (((0,), (0,)), ((), ()))


def _cparams(*sem):
    return pltpu.CompilerParams(dimension_semantics=sem, vmem_limit_bytes=VMEM_LIMIT)


def _silu(y):
    return y * jax.nn.sigmoid(y)


def _softplus(y):
    return jnp.maximum(y, 0.0) + jnp.log1p(jnp.exp(-jnp.abs(y)))


def _normed_rows(x_ref, g_ref, h_scr):
    x = x_ref[...]
    ms = jnp.mean(x * x, axis=-1, keepdims=True)
    h_scr[...] = (x * lax.rsqrt(ms + RMS_EPS) * g_ref[...]).astype(h_scr.dtype)


def _causal_conv(acc, cw_ref, tail_scr, work_scr, j, is_seq_start):
    tm, tn = acc.shape
    taps = cw_ref.shape[0]

    @pl.when(is_seq_start)
    def _():
        tail_scr[j] = jnp.zeros((SUBLANES, tn), F32)

    work_scr[0:SUBLANES, :] = tail_scr[j]
    work_scr[SUBLANES:SUBLANES + tm, :] = acc
    tail_scr[j] = acc[tm - SUBLANES:tm, :]
    y = acc * cw_ref[taps - 1:taps, :]
    for s in range(1, taps):
        y = y + work_scr[SUBLANES - s:SUBLANES - s + tm, :] * cw_ref[taps - 1 - s:taps - s, :]
    return y


def _proj_kernel(*refs, mode, seq, group, l2norm, scale):
    if mode == "conv":
        x_ref, g_ref, w_ref, cw_ref, o_ref, h_scr, tail_scr, work_scr = refs
    elif mode == "rms":
        x_ref, g_ref, w_ref, gn_ref, o_ref, h_scr = refs
    else:
        x_ref, g_ref, w_ref, o_ref, h_scr = refs
    i = pl.program_id(0)
    j = pl.program_id(1)
    tm = x_ref.shape[0]

    @pl.when(j == 0)
    def _():
        _normed_rows(x_ref, g_ref, h_scr)

    acc = jnp.dot(h_scr[...], w_ref[...], preferred_element_type=F32)
    tn = acc.shape[1]
    if mode == "plain":
        y = acc
    elif mode == "silu":
        y = _silu(acc)
    elif mode == "conv":
        y = _silu(_causal_conv(acc, cw_ref, tail_scr, work_scr, j, (i * tm) % seq == 0))
        if l2norm:
            parts = []
            for c in range(tn // group):
                yc = y[:, c * group:(c + 1) * group]
                ss = jnp.sum(yc * yc, axis=-1, keepdims=True)
                parts.append(yc * (lax.rsqrt(ss + L2_EPS) * scale))
            y = jnp.concatenate(parts, axis=-1)
    elif mode == "rms":
        r = lax.broadcasted_iota(jnp.int32, (tn, tn), 0) // group
        c = lax.broadcasted_iota(jnp.int32, (tn, tn), 1) // group
        blk = jnp.where(r == c, 1.0 / group, 0.0).astype(F32)
        ms = jnp.dot(acc * acc, blk, preferred_element_type=F32, precision=HIGHEST)
        y = acc * lax.rsqrt(ms + RMS_EPS) * (gn_ref[...] * scale)
    o_ref[...] = y.astype(o_ref.dtype)


def _proj(x2d, g, w, *, mode, out_dtype, tm, tn, seq, conv_w=None, gain=None, group=LANES, l2norm=False,
          scale=1.0):
    m, d = x2d.shape
    n = w.shape[1]
    tn = min(tn, n)
    assert m % tm == 0 and n % tn == 0 and seq % tm == 0
    in_specs = [pl.BlockSpec((tm, d), lambda i, j: (i, 0)),
                pl.BlockSpec((1, d), lambda i, j: (0, 0)),
                pl.BlockSpec((d, tn), lambda i, j: (0, j))]
    args = [x2d, g.reshape(1, d), w]
    scratch = [pltpu.VMEM((tm, d), BF16)]
    if mode == "conv":
        taps = conv_w.shape[0]
        in_specs.append(pl.BlockSpec((taps, tn), lambda i, j: (0, j)))
        args.append(conv_w)
        scratch += [pltpu.VMEM((n // tn, SUBLANES, tn), F32), pltpu.VMEM((SUBLANES + tm, tn), F32)]
    elif mode == "rms":
        in_specs.append(pl.BlockSpec((1, tn), lambda i, j: (0, j)))
        args.append(gain)
    return pl.pallas_call(
        functools.partial(_proj_kernel, mode=mode, seq=seq, group=group, l2norm=l2norm, scale=scale),
        grid=(m // tm, n // tn),
        in_specs=in_specs,
        out_specs=pl.BlockSpec((tm, tn), lambda i, j: (i, j)),
        out_shape=jax.ShapeDtypeStruct((m, n), out_dtype),
        scratch_shapes=scratch,
        compiler_params=_cparams("arbitrary", "arbitrary"),
        name="proj_" + mode,
    )(*args)


def _out_kernel(a_ref, w_ref, x_ref, o_ref):
    o_ref[...] = x_ref[...] + jnp.dot(a_ref[...], w_ref[...], preferred_element_type=F32)


def _out_proj(a, w, x2d, *, tm):
    m, k = a.shape
    d = w.shape[1]
    return pl.pallas_call(
        _out_kernel,
        grid=(m // tm,),
        in_specs=[pl.BlockSpec((tm, k), lambda i: (i, 0)),
                  pl.BlockSpec((k, d), lambda i: (0, 0)),
                  pl.BlockSpec((tm, d), lambda i: (i, 0))],
        out_specs=pl.BlockSpec((tm, d), lambda i: (i, 0)),
        out_shape=jax.ShapeDtypeStruct((m, d), F32),
        compiler_params=_cparams("arbitrary"),
        name="out_proj",
    )(a, w, x2d)


def _ab_kernel(x_ref, g_ref, w_ref, wt_ref, pc_ref, pr_ref, oc_ref, or_ref, *, heads):
    x = x_ref[...]
    ms = jnp.mean(x * x, axis=-1, keepdims=True)
    h = x * lax.rsqrt(ms + RMS_EPS) * g_ref[...]

    def finish(acc, a_log, dt_bias, idx):
        la = -jnp.exp(a_log) * _softplus(acc + dt_bias)
        return jnp.where(idx < heads, la, jax.nn.sigmoid(acc))

    acc_c = jnp.dot(h, w_ref[...], preferred_element_type=F32, precision=HIGHEST)
    oc_ref[...] = finish(acc_c, pc_ref[0:1, :], pc_ref[1:2, :], lax.broadcasted_iota(jnp.int32, acc_c.shape, 1))
    acc_r = lax.dot_general(wt_ref[...], h, _NT, preferred_element_type=F32, precision=HIGHEST)
    or_ref[...] = finish(acc_r, pr_ref[:, 0:1], pr_ref[:, 1:2], lax.broadcasted_iota(jnp.int32, acc_r.shape, 0))


def _dn_ab(x2d, g, w_ab, a_log, dt_bias, *, tm):
    m, d = x2d.shape
    heads = a_log.shape[0]
    zeros = jnp.zeros((heads,), F32)
    p_cols = jnp.stack([jnp.concatenate([a_log, zeros]), jnp.concatenate([dt_bias, zeros])])
    return pl.pallas_call(
        functools.partial(_ab_kernel, heads=heads),
        grid=(m // tm,),
        in_specs=[pl.BlockSpec((tm, d), lambda i: (i, 0)),
                  pl.BlockSpec((1, d), lambda i: (0, 0)),
                  pl.BlockSpec((d, 2 * heads), lambda i: (0, 0)),
                  pl.BlockSpec((2 * heads, d), lambda i: (0, 0)),
                  pl.BlockSpec((2, 2 * heads), lambda i: (0, 0)),
                  pl.BlockSpec((2 * heads, 2), lambda i: (0, 0))],
        out_specs=[pl.BlockSpec((tm, 2 * heads), lambda i: (i, 0)),
                   pl.BlockSpec((2 * heads, tm), lambda i: (0, i))],
        out_shape=[jax.ShapeDtypeStruct((m, 2 * heads), F32), jax.ShapeDtypeStruct((2 * heads, m), F32)],
        compiler_params=_cparams("arbitrary"),
        name="dn_ab",
    )(x2d, g.reshape(1, d), w_ab, w_ab.T, p_cols, p_cols.T)


def _dn_prep_kernel(q_ref, k_ref, v_ref, abc_ref, abr_ref,
                    u_ref, w_ref, qd_ref, kd_ref, intra_ref, dec_ref, *, heads, chunk):
    h = pl.program_id(1)
    tc = q_ref.shape[1]
    nchunk = tc // chunk
    hp = functools.partial(jnp.dot, preferred_element_type=F32, precision=HIGHEST)

    r = lax.broadcasted_iota(jnp.int32, (tc, tc), 0)
    c = lax.broadcasted_iota(jnp.int32, (tc, tc), 1)
    same = (r // chunk) == (c // chunk)
    tril_blk = jnp.where(same & (c <= r), 1.0, 0.0).astype(F32)
    triu_blk = jnp.where(same & (r <= c), 1.0, 0.0).astype(F32)
    abc = abc_ref[0]
    lane = lax.broadcasted_iota(jnp.int32, abc.shape, 1)
    la_col = jnp.sum(jnp.where(lane == h, abc, 0.0), axis=-1, keepdims=True)
    beta_col = jnp.sum(jnp.where(lane == h + heads, abc, 0.0), axis=-1, keepdims=True)
    g_col = hp(tril_blk, jnp.broadcast_to(la_col, (tc, LANES)))[:, 0:1]
    la_row = abr_ref[pl.ds(h, 1), :]
    g_row = hp(jnp.broadcast_to(la_row, (SUBLANES, tc)), triu_blk)[0:1, :]

    ri = lax.broadcasted_iota(jnp.int32, (chunk, chunk), 0)
    ci = lax.broadcasted_iota(jnp.int32, (chunk, chunk), 1)
    causal = ci <= ri
    strict = ci < ri
    eye = jnp.where(ci == ri, 1.0, 0.0).astype(F32)

    for n in range(nchunk):
        rows = slice(n * chunk, (n + 1) * chunk)
        gc = g_col[rows]
        gr = g_row[:, rows]
        bc = beta_col[rows]
        q = q_ref[0, rows, :]
        k = k_ref[0, rows, :]
        v = v_ref[0, rows, :]
        decay = jnp.where(causal, jnp.exp(jnp.where(causal, gc - gr, 0.0)), 0.0)
        kb = k * bc
        eg = jnp.exp(gc)
        kk = lax.dot_general(kb, k, _NT, preferred_element_type=F32, precision=HIGHEST)
        a = -jnp.where(strict, kk * decay, 0.0)
        tinv = eye + a
        p = a
        steps = max(1, (chunk - 1).bit_length()) - 1
        for _ in range(steps):
            p = hp(p, p)
            tinv = tinv + hp(tinv, p)
        u_ref[0, rows, :] = hp(tinv, v * bc).astype(u_ref.dtype)
        w_ref[0, rows, :] = hp(tinv, kb * eg).astype(w_ref.dtype)
        qk = lax.dot_general(q, k, _NT, preferred_element_type=F32, precision=HIGHEST)
        intra_ref[0, 0, rows, :] = jnp.where(causal, qk * decay, 0.0).astype(intra_ref.dtype)
        qd_ref[0, rows, :] = (q * eg).astype(qd_ref.dtype)
        g_last = gc[chunk - 1:chunk, :]
        kd_ref[0, rows, :] = (k * jnp.exp(g_last - gc)).astype(kd_ref.dtype)
        dec_ref[0, 0, 0, n:n + 1, :] = jnp.broadcast_to(jnp.exp(g_last), (1, dec_ref.shape[-1]))


def _dn_prep(q, k, v, ab_c, ab_r, *, tc):
    b, t, qw = q.shape
    heads = qw // DN_DK
    nt = t // tc
    nchunk = tc // DN_CHUNK
    qk_spec = pl.BlockSpec((1, tc, DN_DK), lambda bi, hi, ti: (bi, ti, hi))
    v_spec = pl.BlockSpec((1, tc, DN_DV), lambda bi, hi, ti: (bi, ti, hi))
    return pl.pallas_call(
        functools.partial(_dn_prep_kernel, heads=heads, chunk=DN_CHUNK),
        grid=(b, heads, nt),
        in_specs=[qk_spec, qk_spec, v_spec,
                  pl.BlockSpec((1, tc, 2 * heads), lambda bi, hi, ti: (bi, ti, 0)),
                  pl.BlockSpec((2 * heads, tc), lambda bi, hi, ti: (0, bi * nt + ti))],
        out_specs=[v_spec, qk_spec, qk_spec, qk_spec,
                   pl.BlockSpec((1, 1, tc, DN_CHUNK), lambda bi, hi, ti: (bi, hi, ti, 0)),
                   pl.BlockSpec((1, 1, 1, nchunk, DN_DV), lambda bi, hi, ti: (bi, hi, ti, 0, 0))],
        out_shape=[jax.ShapeDtypeStruct((b, t, heads * DN_DV), F32),
                   jax.ShapeDtypeStruct((b, t, qw), BF16),
                   jax.ShapeDtypeStruct((b, t, qw), BF16),
                   jax.ShapeDtypeStruct((b, t, qw), BF16),
                   jax.ShapeDtypeStruct((b, heads, t, DN_CHUNK), BF16),
                   jax.ShapeDtypeStruct((b, heads, nt, nchunk, DN_DV), F32)],
        compiler_params=_cparams("arbitrary", "arbitrary", "arbitrary"),
        name="dn_prep",
    )(q, k, v, ab_c, ab_r)


def _dn_scan_kernel(u_ref, w_ref, qd_ref, kd_ref, intra_ref, dec_ref, o_ref, s_scr, *, heads, chunk):
    nchunk = u_ref.shape[1] // chunk

    @pl.when(pl.program_id(1) == 0)
    def _():
        s_scr[...] = jnp.zeros(s_scr.shape, F32)

    def body(n, carry):
        rows = pl.ds(pl.multiple_of(n * chunk, chunk), chunk)
        for h in range(heads):
            kcols = slice(h * DN_DK, (h + 1) * DN_DK)
            vcols = slice(h * DN_DV, (h + 1) * DN_DV)
            s = s_scr[h]
            sb = s.astype(BF16)
            v_new = u_ref[0, rows, vcols] - jnp.dot(w_ref[0, rows, kcols], sb, preferred_element_type=F32)
            vb = v_new.astype(BF16)
            o = (jnp.dot(qd_ref[0, rows, kcols], sb, preferred_element_type=F32)
                 + jnp.dot(intra_ref[0, h, rows, :], vb, preferred_element_type=F32))
            o_ref[0, rows, vcols] = o
            s_scr[h] = (s * dec_ref[0, h, 0, pl.ds(n, 1), :]
                        + lax.dot_general(kd_ref[0, rows, kcols], vb, _TN, preferred_element_type=F32))
        return carry

    lax.fori_loop(0, nchunk, body, 0)


def _dn_scan(u, w, qd, kd, intra, dec, *, tc):
    b, t, vw = u.shape
    heads = vw // DN_DV
    qw = heads * DN_DK
    nt = t // tc
    nchunk = tc // DN_CHUNK
    assert dec.shape == (b, heads, nt, nchunk, DN_DV)
    return pl.pallas_call(
        functools.partial(_dn_scan_kernel, heads=heads, chunk=DN_CHUNK),
        grid=(b, nt),
        in_specs=[pl.BlockSpec((1, tc, vw), lambda bi, ti: (bi, ti, 0)),
                  pl.BlockSpec((1, tc, qw), lambda bi, ti: (bi, ti, 0)),
                  pl.BlockSpec((1, tc, qw), lambda bi, ti: (bi, ti, 0)),
                  pl.BlockSpec((1, tc, qw), lambda bi, ti: (bi, ti, 0)),
                  pl.BlockSpec((1, heads, tc, DN_CHUNK), lambda bi, ti: (bi, 0, ti, 0)),
                  pl.BlockSpec((1, heads, 1, nchunk, DN_DV), lambda bi, ti: (bi, 0, ti, 0, 0))],
        out_specs=pl.BlockSpec((1, tc, vw), lambda bi, ti: (bi, ti, 0)),
        out_shape=jax.ShapeDtypeStruct((b, t, vw), F32),
        scratch_shapes=[pltpu.VMEM((heads, DN_DK, DN_DV), F32)],
        compiler_params=_cparams("arbitrary", "arbitrary"),
        name="dn_scan",
    )(u, w, qd, kd, intra, dec)


def _dn_out_kernel(o_ref, gate_ref, gn_ref, w_ref, x_ref, out_ref, *, heads):
    acc = x_ref[...]
    for h in range(heads):
        cols = slice(h * DN_DV, (h + 1) * DN_DV)
        o = o_ref[:, cols]
        ms = jnp.mean(o * o, axis=-1, keepdims=True)
        y = o * lax.rsqrt(ms + RMS_EPS) * gn_ref[...] * gate_ref[:, cols].astype(F32)
        acc = acc + jnp.dot(y.astype(BF16), w_ref[cols, :], preferred_element_type=F32)
    out_ref[...] = acc


def _dn_out(o2d, gate, gn, w_out, x2d, *, tm):
    m, vw = o2d.shape
    d = x2d.shape[1]
    heads = vw // DN_DV
    return pl.pallas_call(
        functools.partial(_dn_out_kernel, heads=heads),
        grid=(m // tm,),
        in_specs=[pl.BlockSpec((tm, vw), lambda i: (i, 0)),
                  pl.BlockSpec((tm, vw), lambda i: (i, 0)),
                  pl.BlockSpec((1, DN_DV), lambda i: (0, 0)),
                  pl.BlockSpec((vw, d), lambda i: (0, 0)),
                  pl.BlockSpec((tm, d), lambda i: (i, 0))],
        out_specs=pl.BlockSpec((tm, d), lambda i: (i, 0)),
        out_shape=jax.ShapeDtypeStruct((m, d), F32),
        compiler_params=_cparams("arbitrary"),
        name="dn_out",
    )(o2d, gate, gn.reshape(1, DN_DV), w_out, x2d)


def _deltanet_layer(x, g, w_in, conv_w, a_log, dt_bias, o_norm_g, w_out):
    b, t, d = x.shape
    m = b * t
    x2d = x.reshape(m, d)
    heads = a_log.shape[0]
    qkw = heads * DN_DK
    vw = heads * DN_DV
    w_bf = w_in.astype(BF16)
    tm = min(512, t)
    q = _proj(x2d, g, w_bf[:, :qkw], mode="conv", conv_w=conv_w[:, :qkw], out_dtype=F32, tm=tm, tn=512, seq=t,
              group=DN_DK, l2norm=True, scale=DN_DK ** -0.5)
    k = _proj(x2d, g, w_bf[:, qkw:2 * qkw], mode="conv", conv_w=conv_w[:, qkw:2 * qkw], out_dtype=F32, tm=tm,
              tn=512, seq=t, group=DN_DK, l2norm=True)
    v = _proj(x2d, g, w_bf[:, 2 * qkw:2 * qkw + vw], mode="conv", conv_w=conv_w[:, 2 * qkw:], out_dtype=F32,
              tm=tm, tn=512, seq=t)
    gate = _proj(x2d, g, w_bf[:, 2 * qkw + vw:2 * qkw + 2 * vw], mode="silu", out_dtype=BF16, tm=tm, tn=512,
                 seq=t)
    ab_c, ab_r = _dn_ab(x2d, g, w_in[:, 2 * qkw + 2 * vw:], a_log, dt_bias, tm=tm)
    tc = min(256, t)
    u, w, qd, kd, intra, dec = _dn_prep(q.reshape(b, t, qkw), k.reshape(b, t, qkw), v.reshape(b, t, vw),
                                        ab_c.reshape(b, t, 2 * heads), ab_r, tc=tc)
    o = _dn_scan(u, w, qd, kd, intra, dec, tc=tc)
    out = _dn_out(o.reshape(m, vw), gate, o_norm_g, w_out.astype(BF16), x2d, tm=tm)
    return out.reshape(b, t, d)


def _sb_kernel(q_ref, k_ref, v_ref, gate_ref, o_ref, *, dh):
    i = pl.program_id(2)
    tq = q_ref.shape[1]
    tk = tq
    q = q_ref[0]
    lane = lax.broadcasted_iota(jnp.int32, q.shape, 1)
    ri = lax.broadcasted_iota(jnp.int32, (tq, tk), 0)
    ci = lax.broadcasted_iota(jnp.int32, (tq, tk), 1)
    strict = ci < ri
    later = jnp.where(ri > ci, 1.0, 0.0).astype(BF16)

    def block(qm, j, acc, run, masked):
        start = pl.multiple_of(j * tk, tk)
        kj = k_ref[0, pl.ds(start, tk), :]
        vj = v_ref[0, pl.ds(start, tk), :]
        z = lax.dot_general(qm, kj, _NT, preferred_element_type=F32)
        l1m = -_softplus(z)
        if masked:
            l1m = jnp.where(strict, l1m, 0.0)
        hi = l1m.astype(BF16)
        lo = (l1m - hi.astype(F32)).astype(BF16)
        after = (jnp.dot(hi, later, preferred_element_type=F32)
                 + jnp.dot(lo, later, preferred_element_type=F32))
        wts = jnp.exp(z + l1m + after + run)
        if masked:
            wts = jnp.where(strict, wts, 0.0)
        acc = acc + jnp.dot(wts.astype(BF16), vj, preferred_element_type=F32)
        run = run + after[:, 0:1] + l1m[:, 0:1]
        return acc, run

    outs = []
    for hh in range(LANES // dh):
        qm = jnp.where((lane >= hh * dh) & (lane < (hh + 1) * dh), q, jnp.zeros_like(q))
        acc, run = block(qm, i, jnp.zeros((tq, LANES), F32), jnp.zeros((tq, 1), F32), True)

        def body(step, carry, qm=qm):
            return block(qm, i - 1 - step, carry[0], carry[1], False)

        acc, run = lax.fori_loop(0, i, body, (acc, run))
        outs.append(acc)
    o = outs[0]
    for hh in range(1, len(outs)):
        o = jnp.where(lane >= hh * dh, outs[hh], o)
    o_ref[0] = (o * gate_ref[0].astype(F32)).astype(o_ref.dtype)


def _sb_attention(q, k, v, gate, *, tq):
    b, t, w = q.shape
    blk = pl.BlockSpec((1, tq, LANES), lambda bi, hi, ti: (bi, ti, hi))
    full = pl.BlockSpec((1, t, LANES), lambda bi, hi, ti: (bi, 0, hi))
    return pl.pallas_call(
        functools.partial(_sb_kernel, dh=SB_DH),
        grid=(b, w // LANES, t // tq),
        in_specs=[blk, full, full, blk],
        out_specs=blk,
        out_shape=jax.ShapeDtypeStruct((b, t, w), BF16),
        compiler_params=_cparams("arbitrary", "arbitrary", "arbitrary"),
        name="sb_attn",
    )(q, k, v, gate)


def _stickbreak_layer(x, g, w_in, q_norm_g, k_norm_g, w_out):
    b, t, d = x.shape
    m = b * t
    x2d = x.reshape(m, d)
    w = w_in.shape[1] // 4
    heads = w // SB_DH
    w_bf = w_in.astype(BF16)
    tm = min(512, t)
    qg = jnp.tile(q_norm_g, heads).reshape(1, w)
    kg = jnp.tile(k_norm_g, heads).reshape(1, w)
    q = _proj(x2d, g, w_bf[:, :w], mode="rms", gain=qg, out_dtype=BF16, tm=tm, tn=512, seq=t, group=SB_DH,
              scale=SB_DH ** -0.5)
    k = _proj(x2d, g, w_bf[:, w:2 * w], mode="rms", gain=kg, out_dtype=BF16, tm=tm, tn=512, seq=t, group=SB_DH)
    v = _proj(x2d, g, w_bf[:, 2 * w:3 * w], mode="plain", out_dtype=BF16, tm=tm, tn=512, seq=t)
    gate = _proj(x2d, g, w_bf[:, 3 * w:], mode="silu", out_dtype=BF16, tm=tm, tn=512, seq=t)
    o = _sb_attention(q.reshape(b, t, w), k.reshape(b, t, w), v.reshape(b, t, w), gate.reshape(b, t, w),
                      tq=min(128, t))
    out = _out_proj(o.reshape(m, w), w_out.astype(BF16), x2d, tm=tm)
    return out.reshape(b, t, d)


def _sc_kernel(x_ref, g_ref, wb_ref, wc_ref, wu_ref, wg_ref, cw_ref, wo_ref, o_ref,
               h_scr, acc_scr, tail_scr, work_scr, *, seq):
    i = pl.program_id(0)
    j = pl.program_id(1)
    tm = x_ref.shape[0]

    @pl.when(j == 0)
    def _():
        _normed_rows(x_ref, g_ref, h_scr)
        acc_scr[...] = x_ref[...]

    h = h_scr[...]
    cu = (jnp.dot(h, wc_ref[...], preferred_element_type=F32) * jnp.dot(h, wu_ref[...], preferred_element_type=F32))
    y = _causal_conv(cu, cw_ref, tail_scr, work_scr, j, (i * tm) % seq == 0)
    y = y * jnp.dot(h, wb_ref[...], preferred_element_type=F32)
    y = y * _silu(jnp.dot(h, wg_ref[...], preferred_element_type=F32))
    acc_scr[...] += jnp.dot(y.astype(BF16), wo_ref[...], preferred_element_type=F32)

    @pl.when(j == pl.num_programs(1) - 1)
    def _():
        o_ref[...] = acc_scr[...]


def _shortconv_layer(x, g, w_in, conv_w, w_out):
    b, t, d = x.shape
    m = b * t
    x2d = x.reshape(m, d)
    w = w_in.shape[1] // 4
    tm = min(512, t)
    tn = min(512, w)
    nj = w // tn
    w_bf = w_in.astype(BF16)
    taps = conv_w.shape[0]

    def wspec(part):
        return pl.BlockSpec((d, tn), lambda i, j: (0, part * nj + j))

    out = pl.pallas_call(
        functools.partial(_sc_kernel, seq=t),
        grid=(m // tm, nj),
        in_specs=[pl.BlockSpec((tm, d), lambda i, j: (i, 0)),
                  pl.BlockSpec((1, d), lambda i, j: (0, 0)),
                  wspec(0), wspec(1), wspec(2), wspec(3),
                  pl.BlockSpec((taps, tn), lambda i, j: (0, j)),
                  pl.BlockSpec((tn, d), lambda i, j: (j, 0))],
        out_specs=pl.BlockSpec((tm, d), lambda i, j: (i, 0)),
        out_shape=jax.ShapeDtypeStruct((m, d), F32),
        scratch_shapes=[pltpu.VMEM((tm, d), BF16), pltpu.VMEM((tm, d), F32),
                        pltpu.VMEM((nj, SUBLANES, tn), F32), pltpu.VMEM((SUBLANES + tm, tn), F32)],
        compiler_params=_cparams("arbitrary", "arbitrary"),
        name="shortconv_layer",
    )(x2d, g.reshape(1, d), w_bf, w_bf, w_bf, w_bf, conv_w, w_out.astype(BF16))
    return out.reshape(b, t, d)


def kernel(x, norm_g, dn_w_in, dn_conv_w, dn_a_log, dn_dt_bias, dn_o_norm_g, dn_w_out, sb_w_in, sb_q_norm_g,
           sb_k_norm_g, sb_w_out, sc_w_in, sc_conv_w, sc_w_out):
    depth = norm_g.shape[0]
    n_mixers = 3
    for i in range(depth):
        j = i // n_mixers
        kind = i % n_mixers
        if kind == 0:
            x = _deltanet_layer(x, norm_g[i], dn_w_in[j], dn_conv_w[j], dn_a_log[j], dn_dt_bias[j],
                                dn_o_norm_g[j], dn_w_out[j])
        elif kind == 1:
            x = _stickbreak_layer(x, norm_g[i], sb_w_in[j], sb_q_norm_g[j], sb_k_norm_g[j], sb_w_out[j])
        else:
            x = _shortconv_layer(x, norm_g[i], sc_w_in[j], sc_conv_w[j], sc_w_out[j])
    return x
```

```python
import functools

import jax
import jax.numpy as jnp
from jax import lax
from jax.experimental import pallas as pl
from jax.experimental.pallas import tpu as pltpu

F32 = jnp.float32
BF16 = jnp.bfloat16
HIGHEST = lax.Precision.HIGHEST

LOG2E = 1.4426950408889634
RMS_EPS = 1e-6
L2_EPS = 1e-6
DN_HEADS = 8
DN_DK = 128
DN_DV = 256
DN_CHUNK = 64
SB_DH = 64
LANES = 128
SUBLANES = 8
VMEM_LIMIT = 56 * 1024 * 1024

_NT = (((1,), (1,)), ((), ()))
_TN = (((0,), (0,)), ((), ()))


def _cparams(*sem):
    return pltpu.CompilerParams(dimension_semantics=sem, vmem_limit_bytes=VMEM_LIMIT)


def _silu(y):
    return y * jax.nn.sigmoid(y)


def _softplus(y):
    return jnp.maximum(y, 0.0) + jnp.log1p(jnp.exp(-jnp.abs(y)))


def _normed_rows(x_ref, g_ref, h_scr):
    x = x_ref[...]
    ms = jnp.mean(x * x, axis=-1, keepdims=True)
    h_scr[...] = (x * lax.rsqrt(ms + RMS_EPS) * g_ref[...]).astype(h_scr.dtype)


def _causal_conv(acc, cw_ref, tail_scr, work_scr, j, is_seq_start):
    tm, tn = acc.shape
    taps = cw_ref.shape[0]

    @pl.when(is_seq_start)
    def _():
        tail_scr[j] = jnp.zeros((SUBLANES, tn), F32)

    work_scr[0:SUBLANES, :] = tail_scr[j]
    work_scr[SUBLANES:SUBLANES + tm, :] = acc
    tail_scr[j] = acc[tm - SUBLANES:tm, :]
    y = acc * cw_ref[taps - 1:taps, :]
    for s in range(1, taps):
        y = y + work_scr[SUBLANES - s:SUBLANES - s + tm, :] * cw_ref[taps - 1 - s:taps - s, :]
    return y


def _proj_kernel(*refs, mode, seq, group, l2norm, scale):
    if mode == "conv":
        x_ref, g_ref, w_ref, cw_ref, o_ref, h_scr, tail_scr, work_scr = refs
    elif mode == "rms":
        x_ref, g_ref, w_ref, gn_ref, o_ref, h_scr = refs
    else:
        x_ref, g_ref, w_ref, o_ref, h_scr = refs
    i = pl.program_id(0)
    j = pl.program_id(1)
    tm = x_ref.shape[0]

    @pl.when(j == 0)
    def _():
        _normed_rows(x_ref, g_ref, h_scr)

    acc = jnp.dot(h_scr[...], w_ref[...], preferred_element_type=F32)
    tn = acc.shape[1]
    if mode == "plain":
        y = acc
    elif mode == "silu":
        y = _silu(acc)
    elif mode == "conv":
        y = _silu(_causal_conv(acc, cw_ref, tail_scr, work_scr, j, (i * tm) % seq == 0))
        if l2norm:
            parts = []
            for c in range(tn // group):
                yc = y[:, c * group:(c + 1) * group]
                ss = jnp.sum(yc * yc, axis=-1, keepdims=True)
                parts.append(yc * (lax.rsqrt(ss + L2_EPS) * scale))
            y = jnp.concatenate(parts, axis=-1)
    elif mode == "rms":
        r = lax.broadcasted_iota(jnp.int32, (tn, tn), 0) // group
        c = lax.broadcasted_iota(jnp.int32, (tn, tn), 1) // group
        blk = jnp.where(r == c, 1.0 / group, 0.0).astype(F32)
        ms = jnp.dot(acc * acc, blk, preferred_element_type=F32, precision=HIGHEST)
        y = acc * lax.rsqrt(ms + RMS_EPS) * (gn_ref[...] * scale)
    o_ref[...] = y.astype(o_ref.dtype)


def _proj(x2d, g, w, *, mode, out_dtype, tm, tn, seq, conv_w=None, gain=None, group=LANES, l2norm=False,
          scale=1.0):
    m, d = x2d.shape
    n = w.shape[1]
    tn = min(tn, n)
    assert m % tm == 0 and n % tn == 0 and seq % tm == 0
    in_specs = [pl.BlockSpec((tm, d), lambda i, j: (i, 0)),
                pl.BlockSpec((1, d), lambda i, j: (0, 0)),
                pl.BlockSpec((d, tn), lambda i, j: (0, j))]
    args = [x2d, g.reshape(1, d), w]
    scratch = [pltpu.VMEM((tm, d), BF16)]
    if mode == "conv":
        taps = conv_w.shape[0]
        in_specs.append(pl.BlockSpec((taps, tn), lambda i, j: (0, j)))
        args.append(conv_w)
        scratch += [pltpu.VMEM((n // tn, SUBLANES, tn), F32), pltpu.VMEM((SUBLANES + tm, tn), F32)]
    elif mode == "rms":
        in_specs.append(pl.BlockSpec((1, tn), lambda i, j: (0, j)))
        args.append(gain)
    return pl.pallas_call(
        functools.partial(_proj_kernel, mode=mode, seq=seq, group=group, l2norm=l2norm, scale=scale),
        grid=(m // tm, n // tn),
        in_specs=in_specs,
        out_specs=pl.BlockSpec((tm, tn), lambda i, j: (i, j)),
        out_shape=jax.ShapeDtypeStruct((m, n), out_dtype),
        scratch_shapes=scratch,
        compiler_params=_cparams("arbitrary", "arbitrary"),
        name="proj_" + mode,
    )(*args)


def _out_kernel(a_ref, w_ref, x_ref, o_ref):
    o_ref[...] = x_ref[...] + jnp.dot(a_ref[...], w_ref[...], preferred_element_type=F32)


def _out_proj(a, w, x2d, *, tm):
    m, k = a.shape
    d = w.shape[1]
    return pl.pallas_call(
        _out_kernel,
        grid=(m // tm,),
        in_specs=[pl.BlockSpec((tm, k), lambda i: (i, 0)),
                  pl.BlockSpec((k, d), lambda i: (0, 0)),
                  pl.BlockSpec((tm, d), lambda i: (i, 0))],
        out_specs=pl.BlockSpec((tm, d), lambda i: (i, 0)),
        out_shape=jax.ShapeDtypeStruct((m, d), F32),
        compiler_params=_cparams("arbitrary"),
        name="out_proj",
    )(a, w, x2d)


def _ab_kernel(x_ref, g_ref, w_ref, wt_ref, pc_ref, pr_ref, oc_ref, or_ref, *, heads):
    x = x_ref[...]
    ms = jnp.mean(x * x, axis=-1, keepdims=True)
    h = x * lax.rsqrt(ms + RMS_EPS) * g_ref[...]

    def finish(acc, a_log, dt_bias, idx):
        la = -jnp.exp(a_log) * _softplus(acc + dt_bias)
        return jnp.where(idx < heads, la, jax.nn.sigmoid(acc))

    acc_c = jnp.dot(h, w_ref[...], preferred_element_type=F32, precision=HIGHEST)
    oc_ref[...] = finish(acc_c, pc_ref[0:1, :], pc_ref[1:2, :], lax.broadcasted_iota(jnp.int32, acc_c.shape, 1))
    acc_r = lax.dot_general(wt_ref[...], h, _NT, preferred_element_type=F32, precision=HIGHEST)
    or_ref[...] = finish(acc_r, pr_ref[:, 0:1], pr_ref[:, 1:2], lax.broadcasted_iota(jnp.int32, acc_r.shape, 0))


def _dn_ab(x2d, g, w_ab, a_log, dt_bias, *, tm):
    m, d = x2d.shape
    heads = a_log.shape[0]
    zeros = jnp.zeros((heads,), F32)
    p_cols = jnp.stack([jnp.concatenate([a_log, zeros]), jnp.concatenate([dt_bias, zeros])])
    return pl.pallas_call(
        functools.partial(_ab_kernel, heads=heads),
        grid=(m // tm,),
        in_specs=[pl.BlockSpec((tm, d), lambda i: (i, 0)),
                  pl.BlockSpec((1, d), lambda i: (0, 0)),
                  pl.BlockSpec((d, 2 * heads), lambda i: (0, 0)),
                  pl.BlockSpec((2 * heads, d), lambda i: (0, 0)),
                  pl.BlockSpec((2, 2 * heads), lambda i: (0, 0)),
                  pl.BlockSpec((2 * heads, 2), lambda i: (0, 0))],
        out_specs=[pl.BlockSpec((tm, 2 * heads), lambda i: (i, 0)),
                   pl.BlockSpec((2 * heads, tm), lambda i: (0, i))],
        out_shape=[jax.ShapeDtypeStruct((m, 2 * heads), F32), jax.ShapeDtypeStruct((2 * heads, m), F32)],
        compiler_params=_cparams("arbitrary"),
        name="dn_ab",
    )(x2d, g.reshape(1, d), w_ab, w_ab.T, p_cols, p_cols.T)


def _dn_prep_kernel(q_ref, k_ref, v_ref, abc_ref, abr_ref,
                    u_ref, w_ref, qd_ref, kd_ref, intra_ref, dec_ref, *, heads, chunk):
    h = pl.program_id(1)
    tc = q_ref.shape[1]
    nchunk = tc // chunk
    hp = functools.partial(jnp.dot, preferred_element_type=F32, precision=HIGHEST)

    r = lax.broadcasted_iota(jnp.int32, (tc, tc), 0)
    c = lax.broadcasted_iota(jnp.int32, (tc, tc), 1)
    same = (r // chunk) == (c // chunk)
    tril_blk = jnp.where(same & (c <= r), 1.0, 0.0).astype(F32)
    triu_blk = jnp.where(same & (r <= c), 1.0, 0.0).astype(F32)
    abc = abc_ref[0]
    lane = lax.broadcasted_iota(jnp.int32, abc.shape, 1)
    la_col = jnp.sum(jnp.where(lane == h, abc, 0.0), axis=-1, keepdims=True)
    beta_col = jnp.sum(jnp.where(lane == h + heads, abc, 0.0), axis=-1, keepdims=True)
    g_col = hp(tril_blk, jnp.broadcast_to(la_col, (tc, LANES)))[:, 0:1]
    la_row = abr_ref[pl.ds(h, 1), :]
    g_row = hp(jnp.broadcast_to(la_row, (SUBLANES, tc)), triu_blk)[0:1, :]

    ri = lax.broadcasted_iota(jnp.int32, (chunk, chunk), 0)
    ci = lax.broadcasted_iota(jnp.int32, (chunk, chunk), 1)
    causal = ci <= ri
    strict = ci < ri
    eye = jnp.where(ci == ri, 1.0, 0.0).astype(F32)

    for n in range(nchunk):
        rows = slice(n * chunk, (n + 1) * chunk)
        gc = g_col[rows]
        gr = g_row[:, rows]
        bc = beta_col[rows]
        q = q_ref[0, rows, :]
        k = k_ref[0, rows, :]
        v = v_ref[0, rows, :]
        decay = jnp.where(causal, jnp.exp(jnp.where(causal, gc - gr, 0.0)), 0.0)
        kb = k * bc
        eg = jnp.exp(gc)
        kk = lax.dot_general(kb, k, _NT, preferred_element_type=F32, precision=HIGHEST)
        a = -jnp.where(strict, kk * decay, 0.0)
        tinv = eye + a
        p = a
        steps = max(1, (chunk - 1).bit_length()) - 1
        for _ in range(steps):
            p = hp(p, p)
            tinv = tinv + hp(tinv, p)
        u_ref[0, rows, :] = hp(tinv, v * bc).astype(u_ref.dtype)
        w_ref[0, rows, :] = hp(tinv, kb * eg).astype(w_ref.dtype)
        qk = lax.dot_general(q, k, _NT, preferred_element_type=F32, precision=HIGHEST)
        intra_ref[0, 0, rows, :] = jnp.where(causal, qk * decay, 0.0).astype(intra_ref.dtype)
        qd_ref[0, rows, :] = (q * eg).astype(qd_ref.dtype)
        g_last = gc[chunk - 1:chunk, :]
        kd_ref[0, rows, :] = (k * jnp.exp(g_last - gc)).astype(kd_ref.dtype)
        dec_ref[0, 0, 0, n:n + 1, :] = jnp.broadcast_to(jnp.exp(g_last), (1, dec_ref.shape[-1]))


def _dn_prep(q, k, v, ab_c, ab_r, *, tc):
    b, t, qw = q.shape
    heads = qw // DN_DK
    nt = t // tc
    nchunk = tc // DN_CHUNK
    qk_spec = pl.BlockSpec((1, tc, DN_DK), lambda bi, hi, ti: (bi, ti, hi))
    v_spec = pl.BlockSpec((1, tc, DN_DV), lambda bi, hi, ti: (bi, ti, hi))
    return pl.pallas_call(
        functools.partial(_dn_prep_kernel, heads=heads, chunk=DN_CHUNK),
        grid=(b, heads, nt),
        in_specs=[qk_spec, qk_spec, v_spec,
                  pl.BlockSpec((1, tc, 2 * heads), lambda bi, hi, ti: (bi, ti, 0)),
                  pl.BlockSpec((2 * heads, tc), lambda bi, hi, ti: (0, bi * nt + ti))],
        out_specs=[v_spec, qk_spec, qk_spec, qk_spec,
                   pl.BlockSpec((1, 1, tc, DN_CHUNK), lambda bi, hi, ti: (bi, hi, ti, 0)),
                   pl.BlockSpec((1, 1, 1, nchunk, DN_DV), lambda bi, hi, ti: (bi, hi, ti, 0, 0))],
        out_shape=[jax.ShapeDtypeStruct((b, t, heads * DN_DV), F32),
                   jax.ShapeDtypeStruct((b, t, qw), BF16),
                   jax.ShapeDtypeStruct((b, t, qw), BF16),
                   jax.ShapeDtypeStruct((b, t, qw), BF16),
                   jax.ShapeDtypeStruct((b, heads, t, DN_CHUNK), BF16),
                   jax.ShapeDtypeStruct((b, heads, nt, nchunk, DN_DV), F32)],
        compiler_params=_cparams("arbitrary", "arbitrary", "arbitrary"),
        name="dn_prep",
    )(q, k, v, ab_c, ab_r)


def _dn_scan_kernel(u_ref, w_ref, qd_ref, kd_ref, intra_ref, dec_ref, o_ref, s_scr, *, heads, chunk):
    nchunk = u_ref.shape[1] // chunk

    @pl.when(pl.program_id(1) == 0)
    def _():
        s_scr[...] = jnp.zeros(s_scr.shape, F32)

    def body(n, carry):
        rows = pl.ds(pl.multiple_of(n * chunk, chunk), chunk)
        for h in range(heads):
            kcols = slice(h * DN_DK, (h + 1) * DN_DK)
            vcols = slice(h * DN_DV, (h + 1) * DN_DV)
            s = s_scr[h]
            sb = s.astype(BF16)
            v_new = u_ref[0, rows, vcols] - jnp.dot(w_ref[0, rows, kcols], sb, preferred_element_type=F32)
            vb = v_new.astype(BF16)
            o = (jnp.dot(qd_ref[0, rows, kcols], sb, preferred_element_type=F32)
                 + jnp.dot(intra_ref[0, h, rows, :], vb, preferred_element_type=F32))
            o_ref[0, rows, vcols] = o
            s_scr[h] = (s * dec_ref[0, h, 0, pl.ds(n, 1), :]
                        + lax.dot_general(kd_ref[0, rows, kcols], vb, _TN, preferred_element_type=F32))
        return carry

    lax.fori_loop(0, nchunk, body, 0)


def _dn_scan(u, w, qd, kd, intra, dec, *, tc):
    b, t, vw = u.shape
    heads = vw // DN_DV
    qw = heads * DN_DK
    nt = t // tc
    nchunk = tc // DN_CHUNK
    assert dec.shape == (b, heads, nt, nchunk, DN_DV)
    return pl.pallas_call(
        functools.partial(_dn_scan_kernel, heads=heads, chunk=DN_CHUNK),
        grid=(b, nt),
        in_specs=[pl.BlockSpec((1, tc, vw), lambda bi, ti: (bi, ti, 0)),
                  pl.BlockSpec((1, tc, qw), lambda bi, ti: (bi, ti, 0)),
                  pl.BlockSpec((1, tc, qw), lambda bi, ti: (bi, ti, 0)),
                  pl.BlockSpec((1, tc, qw), lambda bi, ti: (bi, ti, 0)),
                  pl.BlockSpec((1, heads, tc, DN_CHUNK), lambda bi, ti: (bi, 0, ti, 0)),
                  pl.BlockSpec((1, heads, 1, nchunk, DN_DV), lambda bi, ti: (bi, 0, ti, 0, 0))],
        out_specs=pl.BlockSpec((1, tc, vw), lambda bi, ti: (bi, ti, 0)),
        out_shape=jax.ShapeDtypeStruct((b, t, vw), F32),
        scratch_shapes=[pltpu.VMEM((heads, DN_DK, DN_DV), F32)],
        compiler_params=_cparams("arbitrary", "arbitrary"),
        name="dn_scan",
    )(u, w, qd, kd, intra, dec)


def _dn_out_kernel(o_ref, gate_ref, gn_ref, w_ref, x_ref, out_ref, *, heads):
    acc = x_ref[...]
    for h in range(heads):
        cols = slice(h * DN_DV, (h + 1) * DN_DV)
        o = o_ref[:, cols]
        ms = jnp.mean(o * o, axis=-1, keepdims=True)
        y = o * lax.rsqrt(ms + RMS_EPS) * gn_ref[...] * gate_ref[:, cols].astype(F32)
        acc = acc + jnp.dot(y.astype(BF16), w_ref[cols, :], preferred_element_type=F32)
    out_ref[...] = acc


def _dn_out(o2d, gate, gn, w_out, x2d, *, tm):
    m, vw = o2d.shape
    d = x2d.shape[1]
    heads = vw // DN_DV
    return pl.pallas_call(
        functools.partial(_dn_out_kernel, heads=heads),
        grid=(m // tm,),
        in_specs=[pl.BlockSpec((tm, vw), lambda i: (i, 0)),
                  pl.BlockSpec((tm, vw), lambda i: (i, 0)),
                  pl.BlockSpec((1, DN_DV), lambda i: (0, 0)),
                  pl.BlockSpec((vw, d), lambda i: (0, 0)),
                  pl.BlockSpec((tm, d), lambda i: (i, 0))],
        out_specs=pl.BlockSpec((tm, d), lambda i: (i, 0)),
        out_shape=jax.ShapeDtypeStruct((m, d), F32),
        compiler_params=_cparams("arbitrary"),
        name="dn_out",
    )(o2d, gate, gn.reshape(1, DN_DV), w_out, x2d)


def _deltanet_layer(x, g, w_in, conv_w, a_log, dt_bias, o_norm_g, w_out):
    b, t, d = x.shape
    m = b * t
    x2d = x.reshape(m, d)
    heads = a_log.shape[0]
    qkw = heads * DN_DK
    vw = heads * DN_DV
    w_bf = w_in.astype(BF16)
    tm = min(512, t)
    q = _proj(x2d, g, w_bf[:, :qkw], mode="conv", conv_w=conv_w[:, :qkw], out_dtype=F32, tm=tm, tn=512, seq=t,
              group=DN_DK, l2norm=True, scale=DN_DK ** -0.5)
    k = _proj(x2d, g, w_bf[:, qkw:2 * qkw], mode="conv", conv_w=conv_w[:, qkw:2 * qkw], out_dtype=F32, tm=tm,
              tn=512, seq=t, group=DN_DK, l2norm=True)
    v = _proj(x2d, g, w_bf[:, 2 * qkw:2 * qkw + vw], mode="conv", conv_w=conv_w[:, 2 * qkw:], out_dtype=F32,
              tm=tm, tn=512, seq=t)
    gate = _proj(x2d, g, w_bf[:, 2 * qkw + vw:2 * qkw + 2 * vw], mode="silu", out_dtype=BF16, tm=tm, tn=512,
                 seq=t)
    ab_c, ab_r = _dn_ab(x2d, g, w_in[:, 2 * qkw + 2 * vw:], a_log, dt_bias, tm=tm)
    tc = min(256, t)
    u, w, qd, kd, intra, dec = _dn_prep(q.reshape(b, t, qkw), k.reshape(b, t, qkw), v.reshape(b, t, vw),
                                        ab_c.reshape(b, t, 2 * heads), ab_r, tc=tc)
    o = _dn_scan(u, w, qd, kd, intra, dec, tc=tc)
    out = _dn_out(o.reshape(m, vw), gate, o_norm_g, w_out.astype(BF16), x2d, tm=tm)
    return out.reshape(b, t, d)


def _sb_kernel(q_ref, k_ref, v_ref, gate_ref, o_ref, *, dh, rows):
    i = pl.program_id(2)
    tq = q_ref.shape[1]
    tk = tq
    nh = LANES // dh
    nlb = q_ref.shape[2] // LANES
    nr = tq // rows
    lane = lax.broadcasted_iota(jnp.int32, (rows, LANES), 1)
    ri = lax.broadcasted_iota(jnp.int32, (tk, tk), 0)
    ci = lax.broadcasted_iota(jnp.int32, (tk, tk), 1)
    later = jnp.where(ri > ci, 1.0, 0.0).astype(BF16)
    qpos = lax.broadcasted_iota(jnp.int32, (rows, tk), 0)
    kpos = lax.broadcasted_iota(jnp.int32, (rows, tk), 1)

    chains = [(lb, r, hh) for lb in range(nlb) for r in range(nr) for hh in range(nh)]
    qms = []
    for lb, r, hh in chains:
        q = q_ref[0, r * rows:(r + 1) * rows, lb * LANES:(lb + 1) * LANES]
        qms.append(jnp.where((lane >= hh * dh) & (lane < (hh + 1) * dh), q, jnp.zeros_like(q)))
    nc = len(chains)

    def block(j, accs, runs, diag):
        start = pl.multiple_of(j * tk, tk)
        kjs = [k_ref[0, pl.ds(start, tk), lb * LANES:(lb + 1) * LANES] for lb in range(nlb)]
        vjs = [v_ref[0, pl.ds(start, tk), lb * LANES:(lb + 1) * LANES] for lb in range(nlb)]
        masks = [kpos < qpos + r * rows if diag else None for _, r, _ in chains]
        zs = [lax.dot_general(qms[c], kjs[chains[c][0]], _NT, preferred_element_type=F32) for c in range(nc)]
        logsigs, l1ms, his, los = [], [], [], []
        for c in range(nc):
            z = zs[c]
            logsig = jnp.minimum(z, 0.0) - jnp.log2(1.0 + jnp.exp2(-jnp.abs(z)))
            l1m = logsig - z
            if diag:
                l1m = jnp.where(masks[c], l1m, 0.0)
            hi = l1m.astype(BF16)
            logsigs.append(logsig)
            l1ms.append(l1m)
            his.append(hi)
            los.append((l1m - hi.astype(F32)).astype(BF16))
        afters = [jnp.dot(his[c], later, preferred_element_type=F32)
                  + jnp.dot(los[c], later, preferred_element_type=F32) for c in range(nc)]
        wts = []
        for c in range(nc):
            w = jnp.exp2(logsigs[c] + afters[c] + runs[c])
            if diag:
                w = jnp.where(masks[c], w, 0.0)
            wts.append(w.astype(BF16))
        new_accs = [accs[c] + jnp.dot(wts[c], vjs[chains[c][0]], preferred_element_type=F32) for c in range(nc)]
        new_runs = [runs[c] + afters[c][:, 0:1] + l1ms[c][:, 0:1] for c in range(nc)]
        return new_accs, new_runs

    accs = [jnp.zeros((rows, LANES), F32)] * nc
    runs = [jnp.zeros((rows, 1), F32)] * nc
    accs, runs = block(i, accs, runs, True)

    def body(step, carry):
        a, r = block(i - 1 - step, list(carry[0]), list(carry[1]), False)
        return tuple(a), tuple(r)

    accs, runs = lax.fori_loop(0, i, body, (tuple(accs), tuple(runs)))
    for c in range(0, nc, nh):
        lb, r, _ = chains[c]
        o = accs[c]
        for hh in range(1, nh):
            o = jnp.where(lane >= hh * dh, accs[c + hh], o)
        rs = slice(r * rows, (r + 1) * rows)
        cs = slice(lb * LANES, (lb + 1) * LANES)
        o_ref[0, rs, cs] = (o * gate_ref[0, rs, cs].astype(F32)).astype(o_ref.dtype)


def _sb_attention(q, k, v, gate, *, tq, bw, rows):
    b, t, w = q.shape
    blk = pl.BlockSpec((1, tq, bw), lambda bi, hi, ti: (bi, ti, hi))
    full = pl.BlockSpec((1, t, bw), lambda bi, hi, ti: (bi, 0, hi))
    return pl.pallas_call(
        functools.partial(_sb_kernel, dh=SB_DH, rows=rows),
        grid=(b, w // bw, t // tq),
        in_specs=[blk, full, full, blk],
        out_specs=blk,
        out_shape=jax.ShapeDtypeStruct((b, t, w), BF16),
        compiler_params=_cparams("arbitrary", "arbitrary", "arbitrary"),
        name="sb_attn",
    )(q, k, v, gate)


def _stickbreak_layer(x, g, w_in, q_norm_g, k_norm_g, w_out):
    b, t, d = x.shape
    m = b * t
    x2d = x.reshape(m, d)
    w = w_in.shape[1] // 4
    heads = w // SB_DH
    w_bf = w_in.astype(BF16)
    tm = min(512, t)
    qg = jnp.tile(q_norm_g, heads).reshape(1, w)
    kg = jnp.tile(k_norm_g, heads).reshape(1, w)
    q = _proj(x2d, g, w_bf[:, :w], mode="rms", gain=qg, out_dtype=BF16, tm=tm, tn=512, seq=t, group=SB_DH,
              scale=SB_DH ** -0.5 * LOG2E)
    k = _proj(x2d, g, w_bf[:, w:2 * w], mode="rms", gain=kg, out_dtype=BF16, tm=tm, tn=512, seq=t, group=SB_DH)
    v = _proj(x2d, g, w_bf[:, 2 * w:3 * w], mode="plain", out_dtype=BF16, tm=tm, tn=512, seq=t)
    gate = _proj(x2d, g, w_bf[:, 3 * w:], mode="silu", out_dtype=BF16, tm=tm, tn=512, seq=t)
    o = _sb_attention(q.reshape(b, t, w), k.reshape(b, t, w), v.reshape(b, t, w), gate.reshape(b, t, w),
                      tq=min(256, t), bw=256, rows=min(256, t))
    out = _out_proj(o.reshape(m, w), w_out.astype(BF16), x2d, tm=tm)
    return out.reshape(b, t, d)


def _sc_kernel(x_ref, g_ref, wb_ref, wc_ref, wu_ref, wg_ref, cw_ref, wo_ref, o_ref,
               h_scr, acc_scr, tail_scr, work_scr, *, seq):
    i = pl.program_id(0)
    j = pl.program_id(1)
    tm = x_ref.shape[0]

    @pl.when(j == 0)
    def _():
        _normed_rows(x_ref, g_ref, h_scr)
        acc_scr[...] = x_ref[...]

    h = h_scr[...]
    cu = (jnp.dot(h, wc_ref[...], preferred_element_type=F32) * jnp.dot(h, wu_ref[...], preferred_element_type=F32))
    y = _causal_conv(cu, cw_ref, tail_scr, work_scr, j, (i * tm) % seq == 0)
    y = y * jnp.dot(h, wb_ref[...], preferred_element_type=F32)
    y = y * _silu(jnp.dot(h, wg_ref[...], preferred_element_type=F32))
    acc_scr[...] += jnp.dot(y.astype(BF16), wo_ref[...], preferred_element_type=F32)

    @pl.when(j == pl.num_programs(1) - 1)
    def _():
        o_ref[...] = acc_scr[...]


def _shortconv_layer(x, g, w_in, conv_w, w_out):
    b, t, d = x.shape
    m = b * t
    x2d = x.reshape(m, d)
    w = w_in.shape[1] // 4
    tm = min(512, t)
    tn = min(512, w)
    nj = w // tn
    w_bf = w_in.astype(BF16)
    taps = conv_w.shape[0]

    def wspec(part):
        return pl.BlockSpec((d, tn), lambda i, j: (0, part * nj + j))

    out = pl.pallas_call(
        functools.partial(_sc_kernel, seq=t),
        grid=(m // tm, nj),
        in_specs=[pl.BlockSpec((tm, d), lambda i, j: (i, 0)),
                  pl.BlockSpec((1, d), lambda i, j: (0, 0)),
                  wspec(0), wspec(1), wspec(2), wspec(3),
                  pl.BlockSpec((taps, tn), lambda i, j: (0, j)),
                  pl.BlockSpec((tn, d), lambda i, j: (j, 0))],
        out_specs=pl.BlockSpec((tm, d), lambda i, j: (i, 0)),
        out_shape=jax.ShapeDtypeStruct((m, d), F32),
        scratch_shapes=[pltpu.VMEM((tm, d), BF16), pltpu.VMEM((tm, d), F32),
                        pltpu.VMEM((nj, SUBLANES, tn), F32), pltpu.VMEM((SUBLANES + tm, tn), F32)],
        compiler_params=_cparams("arbitrary", "arbitrary"),
        name="shortconv_layer",
    )(x2d, g.reshape(1, d), w_bf, w_bf, w_bf, w_bf, conv_w, w_out.astype(BF16))
    return out.reshape(b, t, d)


def kernel(x, norm_g, dn_w_in, dn_conv_w, dn_a_log, dn_dt_bias, dn_o_norm_g, dn_w_out, sb_w_in, sb_q_norm_g,
           sb_k_norm_g, sb_w_out, sc_w_in, sc_conv_w, sc_w_out):
    depth = norm_g.shape[0]
    n_mixers = 3
    for i in range(depth):
        j = i // n_mixers
        kind = i % n_mixers
        if kind == 0:
            x = _deltanet_layer(x, norm_g[i], dn_w_in[j], dn_conv_w[j], dn_a_log[j], dn_dt_bias[j],
                                dn_o_norm_g[j], dn_w_out[j])
        elif kind == 1:
            x = _stickbreak_layer(x, norm_g[i], sb_w_in[j], sb_q_norm_g[j], sb_k_norm_g[j], sb_w_out[j])
        else:
            x = _shortconv_layer(x, norm_g[i], sc_w_in[j], sc_conv_w[j], sc_w_out[j])
    return x
```

```python
import functools

import jax
import jax.numpy as jnp
from jax import lax
from jax.experimental import pallas as pl
from jax.experimental.pallas import tpu as pltpu

F32 = jnp.float32
BF16 = jnp.bfloat16
HIGHEST = lax.Precision.HIGHEST

SIGN_BIT = 0x80000000
RMS_EPS = 1e-6
L2_EPS = 1e-6
DN_HEADS = 8
DN_DK = 128
DN_DV = 256
DN_CHUNK = 64
DN_PREP_PASSES = 3
SB_DH = 64
LANES = 128
SUBLANES = 8
VMEM_LIMIT = 56 * 1024 * 1024

_NT = (((1,), (1,)), ((), ()))
_TN = (((0,), (0,)), ((), ()))


def _cparams(*sem):
    return pltpu.CompilerParams(dimension_semantics=sem, vmem_limit_bytes=VMEM_LIMIT)


def _silu(y):
    return y * jax.nn.sigmoid(y)


def _softplus(y):
    return jnp.maximum(y, 0.0) + jnp.log1p(jnp.exp(-jnp.abs(y)))


def _normed_rows(x_ref, g_ref, h_scr):
    x = x_ref[...]
    ms = jnp.mean(x * x, axis=-1, keepdims=True)
    h_scr[...] = (x * lax.rsqrt(ms + RMS_EPS) * g_ref[...]).astype(h_scr.dtype)


def _causal_conv(acc, cw, tail_ref, work_ref):
    tm = acc.shape[0]
    taps = cw.shape[0]
    work_ref[0:SUBLANES, :] = tail_ref[...]
    work_ref[SUBLANES:SUBLANES + tm, :] = acc
    tail_ref[...] = acc[tm - SUBLANES:tm, :]
    y = acc * cw[taps - 1:taps, :]
    for s in range(1, taps):
        y = y + work_ref[SUBLANES - s:SUBLANES - s + tm, :] * cw[taps - 1 - s:taps - s, :]
    return y


def _group_sum_sq(y, group):
    width = 2 * LANES
    r = lax.broadcasted_iota(jnp.int32, (width, width), 0) // group
    c = lax.broadcasted_iota(jnp.int32, (width, width), 1) // group
    ones = jnp.where(r == c, 1.0, 0.0).astype(BF16)
    sq = (y * y).astype(BF16)
    parts = [jnp.dot(sq[:, s:s + width], ones, preferred_element_type=F32) for s in range(0, y.shape[1], width)]
    return jnp.concatenate(parts, axis=-1)


IN_CHUNK = 512


def _in_proj_kernel(*refs, plan, seq, n_out, n_conv, has_gain):
    refs = list(refs)
    x_ref, g_ref, w_ref = refs[:3]
    pos = 3
    cw_ref = gn_ref = tail_scr = work_scr = None
    if n_conv:
        cw_ref = refs[pos]
        pos += 1
    if has_gain:
        gn_ref = refs[pos]
        pos += 1
    out_refs = refs[pos:pos + n_out]
    if n_conv:
        tail_scr, work_scr = refs[pos + n_out:pos + n_out + 2]
    tm = x_ref.shape[0]
    x = x_ref[...]
    ms = jnp.mean(x * x, axis=-1, keepdims=True)
    h = (x * lax.rsqrt(ms + RMS_EPS) * g_ref[...]).astype(BF16)

    if n_conv:
        @pl.when((pl.program_id(0) * tm) % seq == 0)
        def _():
            tail_scr[...] = jnp.zeros(tail_scr.shape, F32)

    for c, (mode, out, off, group, scale) in enumerate(plan):
        cols = slice(c * IN_CHUNK, (c + 1) * IN_CHUNK)
        acc = jnp.dot(h, w_ref[:, cols], preferred_element_type=F32)
        if mode == "plain":
            y = acc
        elif mode == "silu":
            y = _silu(acc)
        elif mode in ("conv", "conv_l2"):
            y = _silu(_causal_conv(acc, cw_ref[:, cols], tail_scr.at[c], work_scr.at[c]))
            if mode == "conv_l2":
                y = y * (lax.rsqrt(_group_sum_sq(y, group) + L2_EPS) * scale)
        elif mode == "rms":
            y = acc * lax.rsqrt(_group_sum_sq(acc, group) * (1.0 / group) + RMS_EPS) * (gn_ref[:, cols] * scale)
        out_refs[out][:, off:off + IN_CHUNK] = y.astype(out_refs[out].dtype)


def _in_proj(x2d, g, w, plan, out_widths, *, tm, seq, conv_w=None, gain=None):
    m, d = x2d.shape
    n = w.shape[1]
    assert m % tm == 0 and seq % tm == 0 and n == len(plan) * IN_CHUNK
    n_conv = sum(1 for p in plan if p[0].startswith("conv"))
    assert all(p[0].startswith("conv") for p in plan[:n_conv])
    const = lambda i: (0, 0)
    in_specs = [pl.BlockSpec((tm, d), lambda i: (i, 0)), pl.BlockSpec((1, d), const), pl.BlockSpec((d, n), const)]
    args = [x2d, g.reshape(1, d), w]
    scratch = []
    if n_conv:
        in_specs.append(pl.BlockSpec(conv_w.shape, const))
        args.append(conv_w)
        scratch = [pltpu.VMEM((n_conv, SUBLANES, IN_CHUNK), F32), pltpu.VMEM((n_conv, SUBLANES + tm, IN_CHUNK), F32)]
    if gain is not None:
        in_specs.append(pl.BlockSpec(gain.shape, const))
        args.append(gain)
    return pl.pallas_call(
        functools.partial(_in_proj_kernel, plan=tuple(plan), seq=seq, n_out=len(out_widths), n_conv=n_conv,
                          has_gain=gain is not None),
        grid=(m // tm,),
        in_specs=in_specs,
        out_specs=[pl.BlockSpec((tm, ow), lambda i: (i, 0)) for ow in out_widths],
        out_shape=[jax.ShapeDtypeStruct((m, ow), BF16) for ow in out_widths],
        scratch_shapes=scratch,
        compiler_params=_cparams("arbitrary"),
        name="in_proj",
    )(*args)


def _out_kernel(a_ref, w_ref, x_ref, o_ref):
    o_ref[...] = x_ref[...] + jnp.dot(a_ref[...], w_ref[...], preferred_element_type=F32)


def _out_proj(a, w, x2d, *, tm):
    m, k = a.shape
    d = w.shape[1]
    return pl.pallas_call(
        _out_kernel,
        grid=(m // tm,),
        in_specs=[pl.BlockSpec((tm, k), lambda i: (i, 0)),
                  pl.BlockSpec((k, d), lambda i: (0, 0)),
                  pl.BlockSpec((tm, d), lambda i: (i, 0))],
        out_specs=pl.BlockSpec((tm, d), lambda i: (i, 0)),
        out_shape=jax.ShapeDtypeStruct((m, d), F32),
        compiler_params=_cparams("arbitrary"),
        name="out_proj",
    )(a, w, x2d)


def _ab_kernel(x_ref, g_ref, w_ref, wt_ref, pc_ref, pr_ref, oc_ref, or_ref, *, heads, chunk):
    x = x_ref[...]
    tm = x.shape[0]
    ms = jnp.mean(x * x, axis=-1, keepdims=True)
    h = x * lax.rsqrt(ms + RMS_EPS) * g_ref[...]
    r = lax.broadcasted_iota(jnp.int32, (tm, tm), 0)
    c = lax.broadcasted_iota(jnp.int32, (tm, tm), 1)
    same_chunk = (r // chunk) == (c // chunk)

    def finish(acc, a_log, dt_bias):
        return -jnp.exp(a_log) * _softplus(acc + dt_bias), jax.nn.sigmoid(acc)

    acc_c = jnp.dot(h, w_ref[...], preferred_element_type=F32, precision=HIGHEST)
    la, beta = finish(acc_c, pc_ref[0:1, :], pc_ref[1:2, :])
    tril = jnp.where(same_chunk & (c <= r), 1.0, 0.0).astype(F32)
    g_c = jnp.dot(tril, la, preferred_element_type=F32, precision=HIGHEST)
    oc_ref[...] = jnp.where(lax.broadcasted_iota(jnp.int32, acc_c.shape, 1) < heads, g_c, beta)

    acc_r = lax.dot_general(wt_ref[...], h, _NT, preferred_element_type=F32, precision=HIGHEST)
    la, beta = finish(acc_r, pr_ref[:, 0:1], pr_ref[:, 1:2])
    triu = jnp.where(same_chunk & (r <= c), 1.0, 0.0).astype(F32)
    g_r = jnp.dot(la, triu, preferred_element_type=F32, precision=HIGHEST)
    or_ref[...] = jnp.where(lax.broadcasted_iota(jnp.int32, acc_r.shape, 0) < heads, g_r, beta)


def _dn_ab(x2d, g, w_ab, a_log, dt_bias, *, tm):
    m, d = x2d.shape
    heads = a_log.shape[0]
    zeros = jnp.zeros((heads,), F32)
    p_cols = jnp.stack([jnp.concatenate([a_log, zeros]), jnp.concatenate([dt_bias, zeros])])
    return pl.pallas_call(
        functools.partial(_ab_kernel, heads=heads, chunk=DN_CHUNK),
        grid=(m // tm,),
        in_specs=[pl.BlockSpec((tm, d), lambda i: (i, 0)),
                  pl.BlockSpec((1, d), lambda i: (0, 0)),
                  pl.BlockSpec((d, 2 * heads), lambda i: (0, 0)),
                  pl.BlockSpec((2 * heads, d), lambda i: (0, 0)),
                  pl.BlockSpec((2, 2 * heads), lambda i: (0, 0)),
                  pl.BlockSpec((2 * heads, 2), lambda i: (0, 0))],
        out_specs=[pl.BlockSpec((tm, 2 * heads), lambda i: (i, 0)),
                   pl.BlockSpec((2 * heads, tm), lambda i: (0, i))],
        out_shape=[jax.ShapeDtypeStruct((m, 2 * heads), F32), jax.ShapeDtypeStruct((2 * heads, m), F32)],
        compiler_params=_cparams("arbitrary"),
        name="dn_ab",
    )(x2d, g.reshape(1, d), w_ab, w_ab.T, p_cols, p_cols.T)


def _split(a):
    hi = a.astype(BF16)
    return hi, (a - hi.astype(F32)).astype(BF16)


def _mm(a, b, dims, passes):
    dg = functools.partial(lax.dot_general, dimension_numbers=dims, preferred_element_type=F32)
    if passes == 6:
        return dg(a, b, precision=HIGHEST)
    if passes == 1:
        return dg(a.astype(BF16), b.astype(BF16))
    ah, al = _split(a)
    bh, bl = _split(b)
    return dg(ah, bh) + (dg(ah, bl) + dg(al, bh))


_NN = (((1,), (0,)), ((), ()))


def _dn_prep_kernel(q_ref, k_ref, v_ref, gbc_ref, gbr_ref,
                    u_ref, w_ref, qd_ref, kd_ref, intra_ref, dec_ref, *, heads, chunk, passes):
    h = pl.program_id(1)
    tc = q_ref.shape[1]
    units = range(tc // chunk)
    gbc = gbc_ref[0]
    lane = lax.broadcasted_iota(jnp.int32, gbc.shape, 1)
    g_col = jnp.sum(jnp.where(lane == h, gbc, 0.0), axis=-1, keepdims=True)
    beta_col = jnp.sum(jnp.where(lane == h + heads, gbc, 0.0), axis=-1, keepdims=True)
    g_row = gbr_ref[pl.ds(h, 1), :]

    ri = lax.broadcasted_iota(jnp.int32, (chunk, chunk), 0)
    ci = lax.broadcasted_iota(jnp.int32, (chunk, chunk), 1)
    causal = ci <= ri
    strict = ci < ri
    eye = jnp.where(ci == ri, 1.0, 0.0).astype(F32)
    rows = [slice(n * chunk, (n + 1) * chunk) for n in units]

    ks = [k_ref[0, rs, :] for rs in rows]
    qs = [q_ref[0, rs, :] for rs in rows]
    gcs = [g_col[rs] for rs in rows]
    bcs = [beta_col[rs] for rs in rows]
    kbs = [ks[n] * bcs[n] for n in units]
    egs = [jnp.exp(gcs[n]) for n in units]
    decays = [jnp.where(causal, jnp.exp(jnp.where(causal, gcs[n] - g_row[:, rows[n]], 0.0)), 0.0) for n in units]
    kks = [_mm(kbs[n], ks[n], _NT, passes) for n in units]
    qks = [_mm(qs[n], ks[n], _NT, 1) for n in units]
    for n in units:
        intra_ref[0, 0, rows[n], :] = jnp.where(causal, qks[n] * decays[n], 0.0).astype(intra_ref.dtype)
        qd_ref[0, rows[n], :] = (qs[n] * egs[n]).astype(qd_ref.dtype)
        g_last = gcs[n][chunk - 1:chunk, :]
        kd_ref[0, rows[n], :] = (ks[n] * jnp.exp(g_last - gcs[n])).astype(kd_ref.dtype)
        dec_ref[0, 0, 0, n:n + 1, :] = jnp.broadcast_to(jnp.exp(g_last), (1, dec_ref.shape[-1]))

    ls = [jnp.where(strict, kks[n] * decays[n], 0.0) for n in units]

    def quarter(b):
        return ((ri // (2 * b)) == (ci // (2 * b))) & ((ri % (2 * b)) >= b) & ((ci % (2 * b)) < b)

    tinvs = [eye - jnp.where(quarter(1), ls[n], 0.0) for n in units]
    b = 2
    while b < chunk:
        cs = [jnp.where(quarter(b), ls[n], 0.0) for n in units]
        tcs = [_mm(tinvs[n], cs[n], _NN, passes) for n in units]
        tinvs = [tinvs[n] - _mm(tcs[n], tinvs[n], _NN, passes) for n in units]
        b *= 2
    for n in units:
        u_ref[0, rows[n], :] = _mm(tinvs[n], v_ref[0, rows[n], :] * bcs[n], _NN, passes).astype(u_ref.dtype)
    for n in units:
        w_ref[0, rows[n], :] = _mm(tinvs[n], kbs[n] * egs[n], _NN, passes).astype(w_ref.dtype)


def _dn_prep(q, k, v, ab_c, ab_r, *, tc, passes):
    b, t, qw = q.shape
    heads = qw // DN_DK
    nt = t // tc
    nchunk = tc // DN_CHUNK
    qk_spec = pl.BlockSpec((1, tc, DN_DK), lambda bi, hi, ti: (bi, ti, hi))
    v_spec = pl.BlockSpec((1, tc, DN_DV), lambda bi, hi, ti: (bi, ti, hi))
    return pl.pallas_call(
        functools.partial(_dn_prep_kernel, heads=heads, chunk=DN_CHUNK, passes=passes),
        grid=(b, heads, nt),
        in_specs=[qk_spec, qk_spec, v_spec,
                  pl.BlockSpec((1, tc, 2 * heads), lambda bi, hi, ti: (bi, ti, 0)),
                  pl.BlockSpec((2 * heads, tc), lambda bi, hi, ti: (0, bi * nt + ti))],
        out_specs=[v_spec, qk_spec, qk_spec, qk_spec,
                   pl.BlockSpec((1, 1, tc, DN_CHUNK), lambda bi, hi, ti: (bi, hi, ti, 0)),
                   pl.BlockSpec((1, 1, 1, nchunk, DN_DV), lambda bi, hi, ti: (bi, hi, ti, 0, 0))],
        out_shape=[jax.ShapeDtypeStruct((b, t, heads * DN_DV), F32),
                   jax.ShapeDtypeStruct((b, t, qw), BF16),
                   jax.ShapeDtypeStruct((b, t, qw), BF16),
                   jax.ShapeDtypeStruct((b, t, qw), BF16),
                   jax.ShapeDtypeStruct((b, heads, t, DN_CHUNK), BF16),
                   jax.ShapeDtypeStruct((b, heads, nt, nchunk, DN_DV), F32)],
        compiler_params=_cparams("arbitrary", "arbitrary", "arbitrary"),
        name="dn_prep",
    )(q, k, v, ab_c, ab_r)


def _dn_scan_kernel(u_ref, w_ref, qd_ref, kd_ref, intra_ref, dec_ref, o_ref, s_scr, *, heads, chunk):
    nchunk = u_ref.shape[1] // chunk

    @pl.when(pl.program_id(1) == 0)
    def _():
        s_scr[...] = jnp.zeros(s_scr.shape, F32)

    def body(n, carry):
        rows = pl.ds(pl.multiple_of(n * chunk, chunk), chunk)
        for h in range(heads):
            kcols = slice(h * DN_DK, (h + 1) * DN_DK)
            vcols = slice(h * DN_DV, (h + 1) * DN_DV)
            s = s_scr[h]
            sb = s.astype(BF16)
            v_new = u_ref[0, rows, vcols] - jnp.dot(w_ref[0, rows, kcols], sb, preferred_element_type=F32)
            vb = v_new.astype(BF16)
            o = (jnp.dot(qd_ref[0, rows, kcols], sb, preferred_element_type=F32)
                 + jnp.dot(intra_ref[0, h, rows, :], vb, preferred_element_type=F32))
            o_ref[0, rows, vcols] = o
            s_scr[h] = (s * dec_ref[0, h, 0, pl.ds(n, 1), :]
                        + lax.dot_general(kd_ref[0, rows, kcols], vb, _TN, preferred_element_type=F32))
        return carry

    lax.fori_loop(0, nchunk, body, 0)


def _dn_scan(u, w, qd, kd, intra, dec, *, tc):
    b, t, vw = u.shape
    heads = vw // DN_DV
    qw = heads * DN_DK
    nt = t // tc
    nchunk = tc // DN_CHUNK
    assert dec.shape == (b, heads, nt, nchunk, DN_DV)
    return pl.pallas_call(
        functools.partial(_dn_scan_kernel, heads=heads, chunk=DN_CHUNK),
        grid=(b, nt),
        in_specs=[pl.BlockSpec((1, tc, vw), lambda bi, ti: (bi, ti, 0)),
                  pl.BlockSpec((1, tc, qw), lambda bi, ti: (bi, ti, 0)),
                  pl.BlockSpec((1, tc, qw), lambda bi, ti: (bi, ti, 0)),
                  pl.BlockSpec((1, tc, qw), lambda bi, ti: (bi, ti, 0)),
                  pl.BlockSpec((1, heads, tc, DN_CHUNK), lambda bi, ti: (bi, 0, ti, 0)),
                  pl.BlockSpec((1, heads, 1, nchunk, DN_DV), lambda bi, ti: (bi, 0, ti, 0, 0))],
        out_specs=pl.BlockSpec((1, tc, vw), lambda bi, ti: (bi, ti, 0)),
        out_shape=jax.ShapeDtypeStruct((b, t, vw), F32),
        scratch_shapes=[pltpu.VMEM((heads, DN_DK, DN_DV), F32)],
        compiler_params=_cparams("arbitrary", "arbitrary"),
        name="dn_scan",
    )(u, w, qd, kd, intra, dec)


def _dn_out_kernel(o_ref, gate_ref, gn_ref, w_ref, x_ref, out_ref, *, heads):
    acc = x_ref[...]
    for h in range(heads):
        cols = slice(h * DN_DV, (h + 1) * DN_DV)
        o = o_ref[:, cols]
        ms = jnp.mean(o * o, axis=-1, keepdims=True)
        y = o * lax.rsqrt(ms + RMS_EPS) * gn_ref[...] * gate_ref[:, cols].astype(F32)
        acc = acc + jnp.dot(y.astype(BF16), w_ref[cols, :], preferred_element_type=F32)
    out_ref[...] = acc


def _dn_out(o2d, gate, gn, w_out, x2d, *, tm):
    m, vw = o2d.shape
    d = x2d.shape[1]
    heads = vw // DN_DV
    return pl.pallas_call(
        functools.partial(_dn_out_kernel, heads=heads),
        grid=(m // tm,),
        in_specs=[pl.BlockSpec((tm, vw), lambda i: (i, 0)),
                  pl.BlockSpec((tm, vw), lambda i: (i, 0)),
                  pl.BlockSpec((1, DN_DV), lambda i: (0, 0)),
                  pl.BlockSpec((vw, d), lambda i: (0, 0)),
                  pl.BlockSpec((tm, d), lambda i: (i, 0))],
        out_specs=pl.BlockSpec((tm, d), lambda i: (i, 0)),
        out_shape=jax.ShapeDtypeStruct((m, d), F32),
        compiler_params=_cparams("arbitrary"),
        name="dn_out",
    )(o2d, gate, gn.reshape(1, DN_DV), w_out, x2d)


def _deltanet_layer(x, g, w_in, conv_w, a_log, dt_bias, o_norm_g, w_out):
    b, t, d = x.shape
    m = b * t
    x2d = x.reshape(m, d)
    heads = a_log.shape[0]
    qkw = heads * DN_DK
    vw = heads * DN_DV
    tm = min(512, t)
    nq, nv = qkw // IN_CHUNK, vw // IN_CHUNK
    plan = ([("conv_l2", 0, c * IN_CHUNK, DN_DK, DN_DK ** -0.5) for c in range(nq)]
            + [("conv_l2", 1, c * IN_CHUNK, DN_DK, 1.0) for c in range(nq)]
            + [("conv", 2, c * IN_CHUNK, 0, 1.0) for c in range(nv)]
            + [("silu", 3, c * IN_CHUNK, 0, 1.0) for c in range(nv)])
    q, k, v, gate = _in_proj(x2d, g, w_in[:, :2 * qkw + 2 * vw].astype(BF16), plan, (qkw, qkw, vw, vw), tm=tm,
                             seq=t, conv_w=conv_w)
    ab_c, ab_r = _dn_ab(x2d, g, w_in[:, 2 * qkw + 2 * vw:], a_log, dt_bias, tm=tm)
    tc = min(512, t)
    u, w, qd, kd, intra, dec = _dn_prep(q.reshape(b, t, qkw), k.reshape(b, t, qkw), v.reshape(b, t, vw),
                                        ab_c.reshape(b, t, 2 * heads), ab_r, tc=tc, passes=DN_PREP_PASSES)
    o = _dn_scan(u, w, qd, kd, intra, dec, tc=tc)
    out = _dn_out(o.reshape(m, vw), gate, o_norm_g, w_out.astype(BF16), x2d, tm=tm)
    return out.reshape(b, t, d)


def _sb_kernel(q_ref, k_ref, v_ref, gate_ref, o_ref, *, dh, rows):
    i = pl.program_id(2)
    tq = q_ref.shape[1]
    tk = tq
    nh = LANES // dh
    nlb = q_ref.shape[2] // LANES
    nr = tq // rows
    lane = lax.broadcasted_iota(jnp.int32, (rows, LANES), 1)
    ri = lax.broadcasted_iota(jnp.int32, (tk, tk), 0)
    ci = lax.broadcasted_iota(jnp.int32, (tk, tk), 1)
    later = jnp.where(ri >= ci, 1.0, 0.0).astype(BF16)
    qpos = lax.broadcasted_iota(jnp.int32, (rows, tk), 0)
    kpos = lax.broadcasted_iota(jnp.int32, (rows, tk), 1)

    chains = [(lb, r, hh) for lb in range(nlb) for r in range(nr) for hh in range(nh)]
    qms = []
    for lb, r, hh in chains:
        q = q_ref[0, r * rows:(r + 1) * rows, lb * LANES:(lb + 1) * LANES]
        qms.append(jnp.where((lane >= hh * dh) & (lane < (hh + 1) * dh), q, jnp.zeros_like(q)))
    nc = len(chains)

    def scores(j):
        start = pl.multiple_of(j * tk, tk)
        kjs = [k_ref[0, pl.ds(start, tk), lb * LANES:(lb + 1) * LANES] for lb in range(nlb)]
        return [lax.dot_general(qms[c], kjs[chains[c][0]], _NT, preferred_element_type=F32) for c in range(nc)]

    def block(j, zns, accs, runs, diag):
        start = pl.multiple_of(j * tk, tk)
        vjs = [v_ref[0, pl.ds(start, tk), lb * LANES:(lb + 1) * LANES] for lb in range(nlb)]
        masks = [kpos < qpos + r * rows if diag else None for _, r, _ in chains]
        l1ms = []
        for c in range(nc):
            zn = zns[c]
            neg_abs = lax.bitcast_convert_type(lax.bitcast_convert_type(zn, jnp.uint32) | jnp.uint32(SIGN_BIT), F32)
            l1m = jnp.minimum(zn, 0.0) - jnp.log(1.0 + jnp.exp(neg_abs))
            if diag:
                l1m = jnp.where(masks[c], l1m, 0.0)
            l1ms.append(l1m)
        cums = [jnp.dot(l1ms[c].astype(BF16), later, preferred_element_type=F32) for c in range(nc)]
        wts = []
        for c in range(nc):
            w = jnp.exp(cums[c] - zns[c] + runs[c])
            if diag:
                w = jnp.where(masks[c], w, 0.0)
            wts.append(w.astype(BF16))
        new_accs = [accs[c] + jnp.dot(wts[c], vjs[chains[c][0]], preferred_element_type=F32) for c in range(nc)]
        new_runs = [runs[c] + jnp.sum(l1ms[c], axis=-1, keepdims=True) for c in range(nc)]
        return new_accs, new_runs

    zns = scores(i)
    zns_next = scores(jnp.maximum(i - 1, 0))
    accs = [jnp.zeros((rows, LANES), F32)] * nc
    runs = [jnp.zeros((rows, 1), F32)] * nc
    accs, runs = block(i, zns, accs, runs, True)

    def body(step, carry):
        j = i - 1 - step
        nxt = scores(jnp.maximum(j - 1, 0))
        a, r = block(j, list(carry[0]), list(carry[1]), list(carry[2]), False)
        return tuple(nxt), tuple(a), tuple(r)

    _, accs, runs = lax.fori_loop(0, i, body, (tuple(zns_next), tuple(accs), tuple(runs)))
    for c in range(0, nc, nh):
        lb, r, _ = chains[c]
        o = accs[c]
        for hh in range(1, nh):
            o = jnp.where(lane >= hh * dh, accs[c + hh], o)
        rs = slice(r * rows, (r + 1) * rows)
        cs = slice(lb * LANES, (lb + 1) * LANES)
        o_ref[0, rs, cs] = (o * gate_ref[0, rs, cs].astype(F32)).astype(o_ref.dtype)


def _sb_attention(q, k, v, gate, *, tq, bw, rows):
    b, t, w = q.shape
    blk = pl.BlockSpec((1, tq, bw), lambda bi, hi, ti: (bi, ti, hi))
    full = pl.BlockSpec((1, t, bw), lambda bi, hi, ti: (bi, 0, hi))
    return pl.pallas_call(
        functools.partial(_sb_kernel, dh=SB_DH, rows=rows),
        grid=(b, w // bw, t // tq),
        in_specs=[blk, full, full, blk],
        out_specs=blk,
        out_shape=jax.ShapeDtypeStruct((b, t, w), BF16),
        compiler_params=_cparams("arbitrary", "arbitrary", "arbitrary"),
        name="sb_attn",
    )(q, k, v, gate)


def _stickbreak_layer(x, g, w_in, q_norm_g, k_norm_g, w_out):
    b, t, d = x.shape
    m = b * t
    x2d = x.reshape(m, d)
    w = w_in.shape[1] // 4
    heads = w // SB_DH
    tm = min(512, t)
    gains = jnp.concatenate([jnp.tile(q_norm_g, heads), jnp.tile(k_norm_g, heads)]).reshape(1, 2 * w)
    nw = w // IN_CHUNK
    plan = ([("rms", 0, c * IN_CHUNK, SB_DH, -(SB_DH ** -0.5)) for c in range(nw)]
            + [("rms", 1, c * IN_CHUNK, SB_DH, 1.0) for c in range(nw)]
            + [("plain", 2, c * IN_CHUNK, 0, 1.0) for c in range(nw)]
            + [("silu", 3, c * IN_CHUNK, 0, 1.0) for c in range(nw)])
    q, k, v, gate = _in_proj(x2d, g, w_in.astype(BF16), plan, (w, w, w, w), tm=tm, seq=t, gain=gains)
    o = _sb_attention(q.reshape(b, t, w), k.reshape(b, t, w), v.reshape(b, t, w), gate.reshape(b, t, w),
                      tq=min(256, t), bw=256, rows=min(256, t))
    out = _out_proj(o.reshape(m, w), w_out.astype(BF16), x2d, tm=tm)
    return out.reshape(b, t, d)


def _sc_kernel(x_ref, g_ref, wb_ref, wc_ref, wu_ref, wg_ref, cw_ref, wo_ref, o_ref,
               h_scr, acc_scr, tail_scr, work_scr, *, seq):
    i = pl.program_id(0)
    j = pl.program_id(1)
    tm = x_ref.shape[0]

    @pl.when(j == 0)
    def _():
        _normed_rows(x_ref, g_ref, h_scr)
        acc_scr[...] = x_ref[...]

    h = h_scr[...]
    cu = (jnp.dot(h, wc_ref[...], preferred_element_type=F32) * jnp.dot(h, wu_ref[...], preferred_element_type=F32))

    @pl.when(((i * tm) % seq == 0) & (j == 0))
    def _():
        tail_scr[...] = jnp.zeros(tail_scr.shape, F32)

    y = _causal_conv(cu, cw_ref[...], tail_scr.at[j], work_scr)
    y = y * jnp.dot(h, wb_ref[...], preferred_element_type=F32)
    y = y * _silu(jnp.dot(h, wg_ref[...], preferred_element_type=F32))
    acc_scr[...] += jnp.dot(y.astype(BF16), wo_ref[...], preferred_element_type=F32)

    @pl.when(j == pl.num_programs(1) - 1)
    def _():
        o_ref[...] = acc_scr[...]


def _shortconv_layer(x, g, w_in, conv_w, w_out):
    b, t, d = x.shape
    m = b * t
    x2d = x.reshape(m, d)
    w = w_in.shape[1] // 4
    tm = min(512, t)
    tn = min(512, w)
    nj = w // tn
    w_bf = w_in.astype(BF16)
    taps = conv_w.shape[0]

    def wspec(part):
        return pl.BlockSpec((d, tn), lambda i, j: (0, part * nj + j))

    out = pl.pallas_call(
        functools.partial(_sc_kernel, seq=t),
        grid=(m // tm, nj),
        in_specs=[pl.BlockSpec((tm, d), lambda i, j: (i, 0)),
                  pl.BlockSpec((1, d), lambda i, j: (0, 0)),
                  wspec(0), wspec(1), wspec(2), wspec(3),
                  pl.BlockSpec((taps, tn), lambda i, j: (0, j)),
                  pl.BlockSpec((tn, d), lambda i, j: (j, 0))],
        out_specs=pl.BlockSpec((tm, d), lambda i, j: (i, 0)),
        out_shape=jax.ShapeDtypeStruct((m, d), F32),
        scratch_shapes=[pltpu.VMEM((tm, d), BF16), pltpu.VMEM((tm, d), F32),
                        pltpu.VMEM((nj, SUBLANES, tn), F32), pltpu.VMEM((SUBLANES + tm, tn), F32)],
        compiler_params=_cparams("arbitrary", "arbitrary"),
        name="shortconv_layer",
    )(x2d, g.reshape(1, d), w_bf, w_bf, w_bf, w_bf, conv_w, w_out.astype(BF16))
    return out.reshape(b, t, d)


def kernel(x, norm_g, dn_w_in, dn_conv_w, dn_a_log, dn_dt_bias, dn_o_norm_g, dn_w_out, sb_w_in, sb_q_norm_g,
           sb_k_norm_g, sb_w_out, sc_w_in, sc_conv_w, sc_w_out):
    depth = norm_g.shape[0]
    n_mixers = 3
    for i in range(depth):
        j = i // n_mixers
        kind = i % n_mixers
        if kind == 0:
            x = _deltanet_layer(x, norm_g[i], dn_w_in[j], dn_conv_w[j], dn_a_log[j], dn_dt_bias[j],
                                dn_o_norm_g[j], dn_w_out[j])
        elif kind == 1:
            x = _stickbreak_layer(x, norm_g[i], sb_w_in[j], sb_q_norm_g[j], sb_k_norm_g[j], sb_w_out[j])
        else:
            x = _shortconv_layer(x, norm_g[i], sc_w_in[j], sc_conv_w[j], sc_w_out[j])
    return x
```

```python
import functools

import jax
import jax.numpy as jnp
from jax import lax
from jax.experimental import pallas as pl
from jax.experimental.pallas import tpu as pltpu

F32 = jnp.float32
BF16 = jnp.bfloat16
HIGHEST = lax.Precision.HIGHEST

SIGN_BIT = 0x80000000
RMS_EPS = 1e-6
L2_EPS = 1e-6
DN_HEADS = 8
DN_DK = 128
DN_DV = 256
DN_CHUNK = 64
DN_PREP_ROWS = 1024
DN_PREP_PASSES = 1
SB_DH = 64
LANES = 128
SUBLANES = 8
VMEM_LIMIT = 56 * 1024 * 1024

_NT = (((1,), (1,)), ((), ()))
_TN = (((0,), (0,)), ((), ()))


def _cparams(*sem):
    return pltpu.CompilerParams(dimension_semantics=sem, vmem_limit_bytes=VMEM_LIMIT)


def _silu(y):
    return y * jax.nn.sigmoid(y)


def _softplus(y):
    return jnp.maximum(y, 0.0) + jnp.log1p(jnp.exp(-jnp.abs(y)))


def _normed_rows(x_ref, g_ref, h_scr):
    x = x_ref[...]
    ms = jnp.mean(x * x, axis=-1, keepdims=True)
    h_scr[...] = (x * lax.rsqrt(ms + RMS_EPS) * g_ref[...]).astype(h_scr.dtype)


def _causal_conv(acc, cw, tail_ref, work_ref):
    tm = acc.shape[0]
    taps = cw.shape[0]
    work_ref[0:SUBLANES, :] = tail_ref[...]
    work_ref[SUBLANES:SUBLANES + tm, :] = acc
    tail_ref[...] = acc[tm - SUBLANES:tm, :]
    y = acc * cw[taps - 1:taps, :]
    for s in range(1, taps):
        y = y + work_ref[SUBLANES - s:SUBLANES - s + tm, :] * cw[taps - 1 - s:taps - s, :]
    return y


def _group_sum_sq(y, group):
    width = 2 * LANES
    r = lax.broadcasted_iota(jnp.int32, (width, width), 0) // group
    c = lax.broadcasted_iota(jnp.int32, (width, width), 1) // group
    ones = jnp.where(r == c, 1.0, 0.0).astype(BF16)
    sq = (y * y).astype(BF16)
    parts = [jnp.dot(sq[:, s:s + width], ones, preferred_element_type=F32) for s in range(0, y.shape[1], width)]
    return jnp.concatenate(parts, axis=-1)


IN_CHUNK = 512


def _in_proj_kernel(*refs, plan, seq, n_out, n_conv, has_gain):
    refs = list(refs)
    x_ref, g_ref, w_ref = refs[:3]
    pos = 3
    cw_ref = gn_ref = tail_scr = work_scr = None
    if n_conv:
        cw_ref = refs[pos]
        pos += 1
    if has_gain:
        gn_ref = refs[pos]
        pos += 1
    out_refs = refs[pos:pos + n_out]
    if n_conv:
        tail_scr, work_scr = refs[pos + n_out:pos + n_out + 2]
    tm = x_ref.shape[0]
    x = x_ref[...]
    ms = jnp.mean(x * x, axis=-1, keepdims=True)
    h = (x * lax.rsqrt(ms + RMS_EPS) * g_ref[...]).astype(BF16)

    if n_conv:
        @pl.when((pl.program_id(0) * tm) % seq == 0)
        def _():
            tail_scr[...] = jnp.zeros(tail_scr.shape, F32)

    for c, (mode, out, off, group, scale) in enumerate(plan):
        cols = slice(c * IN_CHUNK, (c + 1) * IN_CHUNK)
        acc = jnp.dot(h, w_ref[:, cols], preferred_element_type=F32)
        if mode == "plain":
            y = acc
        elif mode == "silu":
            y = _silu(acc)
        elif mode in ("conv", "conv_l2"):
            y = _silu(_causal_conv(acc, cw_ref[:, cols], tail_scr.at[c], work_scr.at[c]))
            if mode == "conv_l2":
                y = y * (lax.rsqrt(_group_sum_sq(y, group) + L2_EPS) * scale)
        elif mode == "rms":
            y = acc * lax.rsqrt(_group_sum_sq(acc, group) * (1.0 / group) + RMS_EPS) * (gn_ref[:, cols] * scale)
        out_refs[out][:, off:off + IN_CHUNK] = y.astype(out_refs[out].dtype)


def _in_proj(x2d, g, w, plan, out_widths, *, tm, seq, conv_w=None, gain=None):
    m, d = x2d.shape
    n = w.shape[1]
    assert m % tm == 0 and seq % tm == 0 and n == len(plan) * IN_CHUNK
    n_conv = sum(1 for p in plan if p[0].startswith("conv"))
    assert all(p[0].startswith("conv") for p in plan[:n_conv])
    const = lambda i: (0, 0)
    in_specs = [pl.BlockSpec((tm, d), lambda i: (i, 0)), pl.BlockSpec((1, d), const), pl.BlockSpec((d, n), const)]
    args = [x2d, g.reshape(1, d), w]
    scratch = []
    if n_conv:
        in_specs.append(pl.BlockSpec(conv_w.shape, const))
        args.append(conv_w)
        scratch = [pltpu.VMEM((n_conv, SUBLANES, IN_CHUNK), F32), pltpu.VMEM((n_conv, SUBLANES + tm, IN_CHUNK), F32)]
    if gain is not None:
        in_specs.append(pl.BlockSpec(gain.shape, const))
        args.append(gain)
    return pl.pallas_call(
        functools.partial(_in_proj_kernel, plan=tuple(plan), seq=seq, n_out=len(out_widths), n_conv=n_conv,
                          has_gain=gain is not None),
        grid=(m // tm,),
        in_specs=in_specs,
        out_specs=[pl.BlockSpec((tm, ow), lambda i: (i, 0)) for ow in out_widths],
        out_shape=[jax.ShapeDtypeStruct((m, ow), BF16) for ow in out_widths],
        scratch_shapes=scratch,
        compiler_params=_cparams("arbitrary"),
        name="in_proj",
    )(*args)


def _out_kernel(a_ref, w_ref, x_ref, o_ref):
    o_ref[...] = x_ref[...] + jnp.dot(a_ref[...], w_ref[...], preferred_element_type=F32)


def _out_proj(a, w, x2d, *, tm):
    m, k = a.shape
    d = w.shape[1]
    return pl.pallas_call(
        _out_kernel,
        grid=(m // tm,),
        in_specs=[pl.BlockSpec((tm, k), lambda i: (i, 0)),
                  pl.BlockSpec((k, d), lambda i: (0, 0)),
                  pl.BlockSpec((tm, d), lambda i: (i, 0))],
        out_specs=pl.BlockSpec((tm, d), lambda i: (i, 0)),
        out_shape=jax.ShapeDtypeStruct((m, d), F32),
        compiler_params=_cparams("arbitrary"),
        name="out_proj",
    )(a, w, x2d)


def _ab_kernel(x_ref, g_ref, w3_ref, p_ref, oc_ref, or_ref, *, heads, chunk):
    x = x_ref[...]
    tm = x.shape[0]
    ms = jnp.mean(x * x, axis=-1, keepdims=True)
    hh, hl = _split(x * lax.rsqrt(ms + RMS_EPS) * g_ref[...])
    acc = jnp.dot(jnp.concatenate([hh, hl, hh], axis=1), w3_ref[...], preferred_element_type=F32)
    la = -jnp.exp(p_ref[0:1, :]) * _softplus(acc + p_ref[1:2, :])
    beta = jax.nn.sigmoid(acc)
    r = lax.broadcasted_iota(jnp.int32, (tm, tm), 0)
    c = lax.broadcasted_iota(jnp.int32, (tm, tm), 1)
    tril = jnp.where(((r // chunk) == (c // chunk)) & (c <= r), 1.0, 0.0).astype(BF16)
    lah, lal = _split(la)
    g_cum = jnp.dot(jnp.concatenate([tril, tril], axis=1), jnp.concatenate([lah, lal], axis=0),
                    preferred_element_type=F32)
    out = jnp.where(lax.broadcasted_iota(jnp.int32, acc.shape, 1) < heads, g_cum, beta)
    oc_ref[...] = out[:, :2 * heads]
    or_ref[...] = out.T[:2 * heads, :]


def _dn_ab(x2d, g, w_ab, a_log, dt_bias, *, tm):
    m, d = x2d.shape
    heads = a_log.shape[0]
    pad = LANES - 2 * heads
    w_pad = jnp.pad(w_ab, ((0, 0), (0, pad)))
    w_hi = w_pad.astype(BF16)
    w_lo = (w_pad - w_hi.astype(F32)).astype(BF16)
    w3 = jnp.concatenate([w_hi, w_hi, w_lo], axis=0)
    params = jnp.stack([jnp.pad(a_log, (0, LANES - heads)), jnp.pad(dt_bias, (0, LANES - heads))])
    return pl.pallas_call(
        functools.partial(_ab_kernel, heads=heads, chunk=DN_CHUNK),
        grid=(m // tm,),
        in_specs=[pl.BlockSpec((tm, d), lambda i: (i, 0)),
                  pl.BlockSpec((1, d), lambda i: (0, 0)),
                  pl.BlockSpec((3 * d, LANES), lambda i: (0, 0)),
                  pl.BlockSpec((2, LANES), lambda i: (0, 0))],
        out_specs=[pl.BlockSpec((tm, 2 * heads), lambda i: (i, 0)),
                   pl.BlockSpec((2 * heads, tm), lambda i: (0, i))],
        out_shape=[jax.ShapeDtypeStruct((m, 2 * heads), F32), jax.ShapeDtypeStruct((2 * heads, m), F32)],
        compiler_params=_cparams("arbitrary"),
        name="dn_ab",
    )(x2d, g.reshape(1, d), w3, params)


def _split(a):
    hi = a.astype(BF16)
    return hi, (a - hi.astype(F32)).astype(BF16)


def _mm(a, b, dims, passes):
    dg = functools.partial(lax.dot_general, dimension_numbers=dims, preferred_element_type=F32)
    if passes == 6:
        return dg(a, b, precision=HIGHEST)
    if passes == 1:
        return dg(a.astype(BF16), b.astype(BF16))
    ah, al = _split(a)
    bh, bl = _split(b)
    return dg(ah, bh) + (dg(ah, bl) + dg(al, bh))


_NN = (((1,), (0,)), ((), ()))


def _dn_prep_kernel(q_ref, k_ref, v_ref, gbc_ref, gbr_ref,
                    u_ref, w_ref, qd_ref, kd_ref, intra_ref, dec_ref, *, heads, chunk, passes):
    h = pl.program_id(1)
    tc = q_ref.shape[1]
    units = range(tc // chunk)
    gbc = gbc_ref[0]
    lane = lax.broadcasted_iota(jnp.int32, gbc.shape, 1)
    g_col = jnp.sum(jnp.where(lane == h, gbc, 0.0), axis=-1, keepdims=True)
    beta_col = jnp.sum(jnp.where(lane == h + heads, gbc, 0.0), axis=-1, keepdims=True)
    g_row = gbr_ref[pl.ds(h, 1), :]

    ri = lax.broadcasted_iota(jnp.int32, (chunk, chunk), 0)
    ci = lax.broadcasted_iota(jnp.int32, (chunk, chunk), 1)
    causal = ci <= ri
    strict = ci < ri
    eye = jnp.where(ci == ri, 1.0, 0.0).astype(F32)
    rows = [slice(n * chunk, (n + 1) * chunk) for n in units]

    ks = [k_ref[0, rs, :] for rs in rows]
    qs = [q_ref[0, rs, :] for rs in rows]
    gcs = [g_col[rs] for rs in rows]
    bcs = [beta_col[rs] for rs in rows]
    kbs = [ks[n] * bcs[n] for n in units]
    egs = [jnp.exp(gcs[n]) for n in units]
    decays = [jnp.where(causal, jnp.exp(jnp.where(causal, gcs[n] - g_row[:, rows[n]], 0.0)), 0.0) for n in units]
    kks = [_mm(kbs[n], ks[n], _NT, passes) for n in units]
    qks = [_mm(qs[n], ks[n], _NT, 1) for n in units]
    for n in units:
        intra_ref[0, 0, rows[n], :] = jnp.where(causal, qks[n] * decays[n], 0.0).astype(intra_ref.dtype)
        qd_ref[0, rows[n], :] = (qs[n] * egs[n]).astype(qd_ref.dtype)
        g_last = gcs[n][chunk - 1:chunk, :]
        kd_ref[0, rows[n], :] = (ks[n] * jnp.exp(g_last - gcs[n])).astype(kd_ref.dtype)
        grp = dec_ref.shape[3]
        dec_ref[0, 0, n // grp, n % grp:n % grp + 1, :] = jnp.broadcast_to(jnp.exp(g_last), (1, dec_ref.shape[-1]))

    ls = [jnp.where(strict, kks[n] * decays[n], 0.0) for n in units]

    def quarter(b):
        return ((ri // (2 * b)) == (ci // (2 * b))) & ((ri % (2 * b)) >= b) & ((ci % (2 * b)) < b)

    tinvs = [eye - jnp.where(quarter(1), ls[n], 0.0) for n in units]
    b = 2
    while b < chunk:
        cs = [jnp.where(quarter(b), ls[n], 0.0) for n in units]
        tcs = [_mm(tinvs[n], cs[n], _NN, passes) for n in units]
        tinvs = [tinvs[n] - _mm(tcs[n], tinvs[n], _NN, passes) for n in units]
        b *= 2
    for n in units:
        u_ref[0, rows[n], :] = _mm(tinvs[n], v_ref[0, rows[n], :] * bcs[n], _NN, passes).astype(u_ref.dtype)
    for n in units:
        w_ref[0, rows[n], :] = _mm(tinvs[n], kbs[n] * egs[n], _NN, passes).astype(w_ref.dtype)


def _dn_prep(q, k, v, ab_c, ab_r, *, tc, dec_group, passes):
    b, t, qw = q.shape
    heads = qw // DN_DK
    nt = t // tc
    grp = dec_group
    ngrp = tc // (DN_CHUNK * grp)
    qk_spec = pl.BlockSpec((1, tc, DN_DK), lambda bi, hi, ti: (bi, ti, hi))
    v_spec = pl.BlockSpec((1, tc, DN_DV), lambda bi, hi, ti: (bi, ti, hi))
    return pl.pallas_call(
        functools.partial(_dn_prep_kernel, heads=heads, chunk=DN_CHUNK, passes=passes),
        grid=(b, heads, nt),
        in_specs=[qk_spec, qk_spec, v_spec,
                  pl.BlockSpec((1, tc, 2 * heads), lambda bi, hi, ti: (bi, ti, 0)),
                  pl.BlockSpec((2 * heads, tc), lambda bi, hi, ti: (0, bi * nt + ti))],
        out_specs=[v_spec, qk_spec, qk_spec, qk_spec,
                   pl.BlockSpec((1, 1, tc, DN_CHUNK), lambda bi, hi, ti: (bi, hi, ti, 0)),
                   pl.BlockSpec((1, 1, ngrp, grp, DN_DV), lambda bi, hi, ti: (bi, hi, ti, 0, 0))],
        out_shape=[jax.ShapeDtypeStruct((b, t, heads * DN_DV), BF16),
                   jax.ShapeDtypeStruct((b, t, qw), BF16),
                   jax.ShapeDtypeStruct((b, t, qw), BF16),
                   jax.ShapeDtypeStruct((b, t, qw), BF16),
                   jax.ShapeDtypeStruct((b, heads, t, DN_CHUNK), BF16),
                   jax.ShapeDtypeStruct((b, heads, nt * ngrp, grp, DN_DV), F32)],
        compiler_params=_cparams("arbitrary", "arbitrary", "arbitrary"),
        name="dn_prep",
    )(q, k, v, ab_c, ab_r)


def _dn_scan_kernel(u_ref, w_ref, qd_ref, kd_ref, intra_ref, dec_ref, o_ref, s_scr, *, heads, chunk):
    nchunk = u_ref.shape[1] // chunk

    @pl.when(pl.program_id(1) == 0)
    def _():
        s_scr[...] = jnp.zeros(s_scr.shape, F32)

    def body(n, carry):
        rows = pl.ds(pl.multiple_of(n * chunk, chunk), chunk)
        kcols = [slice(h * DN_DK, (h + 1) * DN_DK) for h in range(heads)]
        vcols = [slice(h * DN_DV, (h + 1) * DN_DV) for h in range(heads)]
        ss = [s_scr[h] for h in range(heads)]
        sbs = [s.astype(BF16) for s in ss]
        wq = [jnp.dot(jnp.concatenate([w_ref[0, rows, kcols[h]], qd_ref[0, rows, kcols[h]]], axis=0), sbs[h],
                      preferred_element_type=F32) for h in range(heads)]
        vbs = [(u_ref[0, rows, vcols[h]] - wq[h][:chunk]).astype(BF16) for h in range(heads)]
        for h in range(heads):
            o_ref[0, rows, vcols[h]] = wq[h][chunk:] + jnp.dot(intra_ref[0, h, rows, :], vbs[h],
                                                               preferred_element_type=F32)
        for h in range(heads):
            s_scr[h] = (ss[h] * dec_ref[0, h, 0, pl.ds(n, 1), :]
                        + lax.dot_general(kd_ref[0, rows, kcols[h]], vbs[h], _TN, preferred_element_type=F32))
        return carry

    lax.fori_loop(0, nchunk, body, 0)


def _dn_scan(u, w, qd, kd, intra, dec, *, tc):
    b, t, vw = u.shape
    heads = vw // DN_DV
    qw = heads * DN_DK
    nt = t // tc
    nchunk = tc // DN_CHUNK
    assert dec.shape == (b, heads, nt, nchunk, DN_DV)
    return pl.pallas_call(
        functools.partial(_dn_scan_kernel, heads=heads, chunk=DN_CHUNK),
        grid=(b, nt),
        in_specs=[pl.BlockSpec((1, tc, vw), lambda bi, ti: (bi, ti, 0)),
                  pl.BlockSpec((1, tc, qw), lambda bi, ti: (bi, ti, 0)),
                  pl.BlockSpec((1, tc, qw), lambda bi, ti: (bi, ti, 0)),
                  pl.BlockSpec((1, tc, qw), lambda bi, ti: (bi, ti, 0)),
                  pl.BlockSpec((1, heads, tc, DN_CHUNK), lambda bi, ti: (bi, 0, ti, 0)),
                  pl.BlockSpec((1, heads, 1, nchunk, DN_DV), lambda bi, ti: (bi, 0, ti, 0, 0))],
        out_specs=pl.BlockSpec((1, tc, vw), lambda bi, ti: (bi, ti, 0)),
        out_shape=jax.ShapeDtypeStruct((b, t, vw), F32),
        scratch_shapes=[pltpu.VMEM((heads, DN_DK, DN_DV), F32)],
        compiler_params=_cparams("arbitrary", "arbitrary"),
        name="dn_scan",
    )(u, w, qd, kd, intra, dec)


def _dn_out_kernel(o_ref, gate_ref, gn_ref, w_ref, x_ref, out_ref, *, heads):
    acc = x_ref[...]
    for h in range(heads):
        cols = slice(h * DN_DV, (h + 1) * DN_DV)
        o = o_ref[:, cols]
        ms = jnp.mean(o * o, axis=-1, keepdims=True)
        y = o * lax.rsqrt(ms + RMS_EPS) * gn_ref[...] * gate_ref[:, cols].astype(F32)
        acc = acc + jnp.dot(y.astype(BF16), w_ref[cols, :], preferred_element_type=F32)
    out_ref[...] = acc


def _dn_out(o2d, gate, gn, w_out, x2d, *, tm):
    m, vw = o2d.shape
    d = x2d.shape[1]
    heads = vw // DN_DV
    return pl.pallas_call(
        functools.partial(_dn_out_kernel, heads=heads),
        grid=(m // tm,),
        in_specs=[pl.BlockSpec((tm, vw), lambda i: (i, 0)),
                  pl.BlockSpec((tm, vw), lambda i: (i, 0)),
                  pl.BlockSpec((1, DN_DV), lambda i: (0, 0)),
                  pl.BlockSpec((vw, d), lambda i: (0, 0)),
                  pl.BlockSpec((tm, d), lambda i: (i, 0))],
        out_specs=pl.BlockSpec((tm, d), lambda i: (i, 0)),
        out_shape=jax.ShapeDtypeStruct((m, d), F32),
        compiler_params=_cparams("arbitrary"),
        name="dn_out",
    )(o2d, gate, gn.reshape(1, DN_DV), w_out, x2d)


def _deltanet_layer(x, g, w_in, conv_w, a_log, dt_bias, o_norm_g, w_out):
    b, t, d = x.shape
    m = b * t
    x2d = x.reshape(m, d)
    heads = a_log.shape[0]
    qkw = heads * DN_DK
    vw = heads * DN_DV
    tm = min(512, t)
    nq, nv = qkw // IN_CHUNK, vw // IN_CHUNK
    plan = ([("conv_l2", 0, c * IN_CHUNK, DN_DK, DN_DK ** -0.5) for c in range(nq)]
            + [("conv_l2", 1, c * IN_CHUNK, DN_DK, 1.0) for c in range(nq)]
            + [("conv", 2, c * IN_CHUNK, 0, 1.0) for c in range(nv)]
            + [("silu", 3, c * IN_CHUNK, 0, 1.0) for c in range(nv)])
    q, k, v, gate = _in_proj(x2d, g, w_in[:, :2 * qkw + 2 * vw].astype(BF16), plan, (qkw, qkw, vw, vw), tm=tm,
                             seq=t, conv_w=conv_w)
    ab_c, ab_r = _dn_ab(x2d, g, w_in[:, 2 * qkw + 2 * vw:], a_log, dt_bias, tm=tm)
    tc = min(512, t)
    u, w, qd, kd, intra, dec = _dn_prep(q.reshape(b, t, qkw), k.reshape(b, t, qkw), v.reshape(b, t, vw),
                                        ab_c.reshape(b, t, 2 * heads), ab_r, tc=min(DN_PREP_ROWS, t),
                                        dec_group=tc // DN_CHUNK, passes=DN_PREP_PASSES)
    o = _dn_scan(u, w, qd, kd, intra, dec, tc=tc)
    out = _dn_out(o.reshape(m, vw), gate, o_norm_g, w_out.astype(BF16), x2d, tm=tm)
    return out.reshape(b, t, d)


def _sb_kernel(q_ref, k_ref, v_ref, gate_ref, o_ref, *, dh, rows):
    i = pl.program_id(2)
    tq = q_ref.shape[1]
    tk = tq
    nh = LANES // dh
    nlb = q_ref.shape[2] // LANES
    nr = tq // rows
    lane = lax.broadcasted_iota(jnp.int32, (rows, LANES), 1)
    ri = lax.broadcasted_iota(jnp.int32, (tk, tk), 0)
    ci = lax.broadcasted_iota(jnp.int32, (tk, tk), 1)
    later = jnp.where(ri >= ci, 1.0, 0.0).astype(BF16)
    qpos = lax.broadcasted_iota(jnp.int32, (rows, tk), 0)
    kpos = lax.broadcasted_iota(jnp.int32, (rows, tk), 1)

    chains = [(lb, r, hh) for lb in range(nlb) for r in range(nr) for hh in range(nh)]
    qms = []
    for lb, r, hh in chains:
        q = q_ref[0, r * rows:(r + 1) * rows, lb * LANES:(lb + 1) * LANES]
        qms.append(jnp.where((lane >= hh * dh) & (lane < (hh + 1) * dh), q, jnp.zeros_like(q)))
    nc = len(chains)

    def scores(j):
        start = pl.multiple_of(j * tk, tk)
        kjs = [k_ref[0, pl.ds(start, tk), lb * LANES:(lb + 1) * LANES] for lb in range(nlb)]
        return [lax.dot_general(qms[c], kjs[chains[c][0]], _NT, preferred_element_type=F32) for c in range(nc)]

    def block(j, zns, accs, runs, diag):
        start = pl.multiple_of(j * tk, tk)
        vjs = [v_ref[0, pl.ds(start, tk), lb * LANES:(lb + 1) * LANES] for lb in range(nlb)]
        masks = [kpos < qpos + r * rows if diag else None for _, r, _ in chains]
        l1ms = []
        for c in range(nc):
            zn = zns[c]
            neg_abs = lax.bitcast_convert_type(lax.bitcast_convert_type(zn, jnp.uint32) | jnp.uint32(SIGN_BIT), F32)
            l1m = jnp.minimum(zn, 0.0) - jnp.log(1.0 + jnp.exp(neg_abs))
            if diag:
                l1m = jnp.where(masks[c], l1m, 0.0)
            l1ms.append(l1m)
        cums = [jnp.dot(l1ms[c].astype(BF16), later, preferred_element_type=F32) for c in range(nc)]
        wts = []
        for c in range(nc):
            w = jnp.exp(cums[c] - zns[c] + runs[c])
            if diag:
                w = jnp.where(masks[c], w, 0.0)
            wts.append(w.astype(BF16))
        new_accs = [accs[c] + jnp.dot(wts[c], vjs[chains[c][0]], preferred_element_type=F32) for c in range(nc)]
        new_runs = [runs[c] + jnp.sum(l1ms[c], axis=-1, keepdims=True) for c in range(nc)]
        return new_accs, new_runs

    zns = scores(i)
    zns_next = scores(jnp.maximum(i - 1, 0))
    accs = [jnp.zeros((rows, LANES), F32)] * nc
    runs = [jnp.zeros((rows, 1), F32)] * nc
    accs, runs = block(i, zns, accs, runs, True)

    def body(step, carry):
        j = i - 1 - step
        nxt = scores(jnp.maximum(j - 1, 0))
        a, r = block(j, list(carry[0]), list(carry[1]), list(carry[2]), False)
        return tuple(nxt), tuple(a), tuple(r)

    _, accs, runs = lax.fori_loop(0, i, body, (tuple(zns_next), tuple(accs), tuple(runs)))
    for c in range(0, nc, nh):
        lb, r, _ = chains[c]
        o = accs[c]
        for hh in range(1, nh):
            o = jnp.where(lane >= hh * dh, accs[c + hh], o)
        rs = slice(r * rows, (r + 1) * rows)
        cs = slice(lb * LANES, (lb + 1) * LANES)
        o_ref[0, rs, cs] = (o * gate_ref[0, rs, cs].astype(F32)).astype(o_ref.dtype)


def _sb_attention(q, k, v, gate, *, tq, bw, rows):
    b, t, w = q.shape
    blk = pl.BlockSpec((1, tq, bw), lambda bi, hi, ti: (bi, ti, hi))
    full = pl.BlockSpec((1, t, bw), lambda bi, hi, ti: (bi, 0, hi))
    return pl.pallas_call(
        functools.partial(_sb_kernel, dh=SB_DH, rows=rows),
        grid=(b, w // bw, t // tq),
        in_specs=[blk, full, full, blk],
        out_specs=blk,
        out_shape=jax.ShapeDtypeStruct((b, t, w), BF16),
        compiler_params=_cparams("arbitrary", "arbitrary", "arbitrary"),
        name="sb_attn",
    )(q, k, v, gate)


def _stickbreak_layer(x, g, w_in, q_norm_g, k_norm_g, w_out):
    b, t, d = x.shape
    m = b * t
    x2d = x.reshape(m, d)
    w = w_in.shape[1] // 4
    heads = w // SB_DH
    tm = min(512, t)
    gains = jnp.concatenate([jnp.tile(q_norm_g, heads), jnp.tile(k_norm_g, heads)]).reshape(1, 2 * w)
    nw = w // IN_CHUNK
    plan = ([("rms", 0, c * IN_CHUNK, SB_DH, -(SB_DH ** -0.5)) for c in range(nw)]
            + [("rms", 1, c * IN_CHUNK, SB_DH, 1.0) for c in range(nw)]
            + [("plain", 2, c * IN_CHUNK, 0, 1.0) for c in range(nw)]
            + [("silu", 3, c * IN_CHUNK, 0, 1.0) for c in range(nw)])
    q, k, v, gate = _in_proj(x2d, g, w_in.astype(BF16), plan, (w, w, w, w), tm=tm, seq=t, gain=gains)
    o = _sb_attention(q.reshape(b, t, w), k.reshape(b, t, w), v.reshape(b, t, w), gate.reshape(b, t, w),
                      tq=min(256, t), bw=256, rows=min(256, t))
    out = _out_proj(o.reshape(m, w), w_out.astype(BF16), x2d, tm=tm)
    return out.reshape(b, t, d)


def _sc_kernel(x_ref, g_ref, wb_ref, wc_ref, wu_ref, wg_ref, cw_ref, wo_ref, o_ref,
               h_scr, acc_scr, tail_scr, work_scr, *, seq):
    i = pl.program_id(0)
    j = pl.program_id(1)
    tm = x_ref.shape[0]

    @pl.when(j == 0)
    def _():
        _normed_rows(x_ref, g_ref, h_scr)
        acc_scr[...] = x_ref[...]

    h = h_scr[...]
    cu = (jnp.dot(h, wc_ref[...], preferred_element_type=F32) * jnp.dot(h, wu_ref[...], preferred_element_type=F32))

    @pl.when(((i * tm) % seq == 0) & (j == 0))
    def _():
        tail_scr[...] = jnp.zeros(tail_scr.shape, F32)

    y = _causal_conv(cu, cw_ref[...], tail_scr.at[j], work_scr)
    y = y * jnp.dot(h, wb_ref[...], preferred_element_type=F32)
    y = y * _silu(jnp.dot(h, wg_ref[...], preferred_element_type=F32))
    acc_scr[...] += jnp.dot(y.astype(BF16), wo_ref[...], preferred_element_type=F32)

    @pl.when(j == pl.num_programs(1) - 1)
    def _():
        o_ref[...] = acc_scr[...]


def _shortconv_layer(x, g, w_in, conv_w, w_out):
    b, t, d = x.shape
    m = b * t
    x2d = x.reshape(m, d)
    w = w_in.shape[1] // 4
    tm = min(512, t)
    tn = min(512, w)
    nj = w // tn
    w_bf = w_in.astype(BF16)
    taps = conv_w.shape[0]

    def wspec(part):
        return pl.BlockSpec((d, tn), lambda i, j: (0, part * nj + j))

    out = pl.pallas_call(
        functools.partial(_sc_kernel, seq=t),
        grid=(m // tm, nj),
        in_specs=[pl.BlockSpec((tm, d), lambda i, j: (i, 0)),
                  pl.BlockSpec((1, d), lambda i, j: (0, 0)),
                  wspec(0), wspec(1), wspec(2), wspec(3),
                  pl.BlockSpec((taps, tn), lambda i, j: (0, j)),
                  pl.BlockSpec((tn, d), lambda i, j: (j, 0))],
        out_specs=pl.BlockSpec((tm, d), lambda i, j: (i, 0)),
        out_shape=jax.ShapeDtypeStruct((m, d), F32),
        scratch_shapes=[pltpu.VMEM((tm, d), BF16), pltpu.VMEM((tm, d), F32),
                        pltpu.VMEM((nj, SUBLANES, tn), F32), pltpu.VMEM((SUBLANES + tm, tn), F32)],
        compiler_params=_cparams("arbitrary", "arbitrary"),
        name="shortconv_layer",
    )(x2d, g.reshape(1, d), w_bf, w_bf, w_bf, w_bf, conv_w, w_out.astype(BF16))
    return out.reshape(b, t, d)


def kernel(x, norm_g, dn_w_in, dn_conv_w, dn_a_log, dn_dt_bias, dn_o_norm_g, dn_w_out, sb_w_in, sb_q_norm_g,
           sb_k_norm_g, sb_w_out, sc_w_in, sc_conv_w, sc_w_out):
    depth = norm_g.shape[0]
    n_mixers = 3
    for i in range(depth):
        j = i // n_mixers
        kind = i % n_mixers
        if kind == 0:
            x = _deltanet_layer(x, norm_g[i], dn_w_in[j], dn_conv_w[j], dn_a_log[j], dn_dt_bias[j],
                                dn_o_norm_g[j], dn_w_out[j])
        elif kind == 1:
            x = _stickbreak_layer(x, norm_g[i], sb_w_in[j], sb_q_norm_g[j], sb_k_norm_g[j], sb_w_out[j])
        else:
            x = _shortconv_layer(x, norm_g[i], sc_w_in[j], sc_conv_w[j], sc_w_out[j])
    return x
```

```python
import functools

import jax
import jax.numpy as jnp
from jax import lax
from jax.experimental import pallas as pl
from jax.experimental.pallas import tpu as pltpu

F32 = jnp.float32
BF16 = jnp.bfloat16
HIGHEST = lax.Precision.HIGHEST

SIGN_BIT = 0x80000000
RMS_EPS = 1e-6
L2_EPS = 1e-6
DN_HEADS = 8
DN_DK = 128
DN_DV = 256
DN_CHUNK = 64
DN_PREP_ROWS = 1024
DN_PREP_PASSES = 1
SB_DH = 64
LANES = 128
SUBLANES = 8
VMEM_LIMIT = 56 * 1024 * 1024

_NT = (((1,), (1,)), ((), ()))
_TN = (((0,), (0,)), ((), ()))


def _cparams(*sem):
    return pltpu.CompilerParams(dimension_semantics=sem, vmem_limit_bytes=VMEM_LIMIT)


def _silu(y):
    return y * jax.nn.sigmoid(y)


def _softplus(y):
    return jnp.maximum(y, 0.0) + jnp.log1p(jnp.exp(-jnp.abs(y)))


def _normed_rows(x_ref, g_ref, h_scr):
    x = x_ref[...]
    ms = jnp.mean(x * x, axis=-1, keepdims=True)
    h_scr[...] = (x * lax.rsqrt(ms + RMS_EPS) * g_ref[...]).astype(h_scr.dtype)


def _causal_conv(acc, cw, tail_ref, work_ref):
    tm = acc.shape[0]
    taps = cw.shape[0]
    work_ref[0:SUBLANES, :] = tail_ref[...]
    work_ref[SUBLANES:SUBLANES + tm, :] = acc
    tail_ref[...] = acc[tm - SUBLANES:tm, :]
    y = acc * cw[taps - 1:taps, :]
    for s in range(1, taps):
        y = y + work_ref[SUBLANES - s:SUBLANES - s + tm, :] * cw[taps - 1 - s:taps - s, :]
    return y


def _group_sum_sq(y, group):
    width = 2 * LANES
    r = lax.broadcasted_iota(jnp.int32, (width, width), 0) // group
    c = lax.broadcasted_iota(jnp.int32, (width, width), 1) // group
    ones = jnp.where(r == c, 1.0, 0.0).astype(BF16)
    sq = (y * y).astype(BF16)
    parts = [jnp.dot(sq[:, s:s + width], ones, preferred_element_type=F32) for s in range(0, y.shape[1], width)]
    return jnp.concatenate(parts, axis=-1)


IN_CHUNK = 512


def _in_proj_kernel(*refs, plan, seq, n_out, n_conv, has_gain):
    refs = list(refs)
    x_ref, g_ref, w_ref = refs[:3]
    pos = 3
    cw_ref = gn_ref = tail_scr = work_scr = None
    if n_conv:
        cw_ref = refs[pos]
        pos += 1
    if has_gain:
        gn_ref = refs[pos]
        pos += 1
    out_refs = refs[pos:pos + n_out]
    if n_conv:
        tail_scr, work_scr = refs[pos + n_out:pos + n_out + 2]
    tm = x_ref.shape[0]
    x = x_ref[...]
    ms = jnp.mean(x * x, axis=-1, keepdims=True)
    h = (x * lax.rsqrt(ms + RMS_EPS) * g_ref[...]).astype(BF16)

    if n_conv:
        @pl.when((pl.program_id(0) * tm) % seq == 0)
        def _():
            tail_scr[...] = jnp.zeros(tail_scr.shape, F32)

    for c, (mode, out, off, group, scale) in enumerate(plan):
        cols = slice(c * IN_CHUNK, (c + 1) * IN_CHUNK)
        acc = jnp.dot(h, w_ref[:, cols], preferred_element_type=F32)
        if mode == "plain":
            y = acc
        elif mode == "silu":
            y = _silu(acc)
        elif mode in ("conv", "conv_l2"):
            y = _silu(_causal_conv(acc, cw_ref[:, cols], tail_scr.at[c], work_scr.at[c]))
            if mode == "conv_l2":
                y = y * (lax.rsqrt(_group_sum_sq(y, group) + L2_EPS) * scale)
        elif mode == "rms":
            y = acc * lax.rsqrt(_group_sum_sq(acc, group) * (1.0 / group) + RMS_EPS) * (gn_ref[:, cols] * scale)
        out_refs[out][:, off:off + IN_CHUNK] = y.astype(out_refs[out].dtype)


def _in_proj(x2d, g, w, plan, out_widths, *, tm, seq, conv_w=None, gain=None):
    m, d = x2d.shape
    n = w.shape[1]
    assert m % tm == 0 and seq % tm == 0 and n == len(plan) * IN_CHUNK
    n_conv = sum(1 for p in plan if p[0].startswith("conv"))
    assert all(p[0].startswith("conv") for p in plan[:n_conv])
    const = lambda i: (0, 0)
    in_specs = [pl.BlockSpec((tm, d), lambda i: (i, 0)), pl.BlockSpec((1, d), const), pl.BlockSpec((d, n), const)]
    args = [x2d, g.reshape(1, d), w]
    scratch = []
    if n_conv:
        in_specs.append(pl.BlockSpec(conv_w.shape, const))
        args.append(conv_w)
        scratch = [pltpu.VMEM((n_conv, SUBLANES, IN_CHUNK), F32), pltpu.VMEM((n_conv, SUBLANES + tm, IN_CHUNK), F32)]
    if gain is not None:
        in_specs.append(pl.BlockSpec(gain.shape, const))
        args.append(gain)
    return pl.pallas_call(
        functools.partial(_in_proj_kernel, plan=tuple(plan), seq=seq, n_out=len(out_widths), n_conv=n_conv,
                          has_gain=gain is not None),
        grid=(m // tm,),
        in_specs=in_specs,
        out_specs=[pl.BlockSpec((tm, ow), lambda i: (i, 0)) for ow in out_widths],
        out_shape=[jax.ShapeDtypeStruct((m, ow), BF16) for ow in out_widths],
        scratch_shapes=scratch,
        compiler_params=_cparams("arbitrary"),
        name="in_proj",
    )(*args)


def _out_kernel(a_ref, w_ref, x_ref, o_ref):
    o_ref[...] = x_ref[...] + jnp.dot(a_ref[...], w_ref[...], preferred_element_type=F32)


def _out_proj(a, w, x2d, *, tm):
    m, k = a.shape
    d = w.shape[1]
    return pl.pallas_call(
        _out_kernel,
        grid=(m // tm,),
        in_specs=[pl.BlockSpec((tm, k), lambda i: (i, 0)),
                  pl.BlockSpec((k, d), lambda i: (0, 0)),
                  pl.BlockSpec((tm, d), lambda i: (i, 0))],
        out_specs=pl.BlockSpec((tm, d), lambda i: (i, 0)),
        out_shape=jax.ShapeDtypeStruct((m, d), F32),
        compiler_params=_cparams("arbitrary"),
        name="out_proj",
    )(a, w, x2d)


def _ab_kernel(x_ref, g_ref, w3_ref, p_ref, oc_ref, or_ref, *, heads, chunk):
    x = x_ref[...]
    tm = x.shape[0]
    ms = jnp.mean(x * x, axis=-1, keepdims=True)
    hh, hl = _split(x * lax.rsqrt(ms + RMS_EPS) * g_ref[...])
    acc = jnp.dot(jnp.concatenate([hh, hl, hh], axis=1), w3_ref[...], preferred_element_type=F32)
    la = -jnp.exp(p_ref[0:1, :]) * _softplus(acc + p_ref[1:2, :])
    beta = jax.nn.sigmoid(acc)
    r = lax.broadcasted_iota(jnp.int32, (tm, tm), 0)
    c = lax.broadcasted_iota(jnp.int32, (tm, tm), 1)
    tril = jnp.where(((r // chunk) == (c // chunk)) & (c <= r), 1.0, 0.0).astype(BF16)
    lah, lal = _split(la)
    g_cum = jnp.dot(jnp.concatenate([tril, tril], axis=1), jnp.concatenate([lah, lal], axis=0),
                    preferred_element_type=F32)
    out = jnp.where(lax.broadcasted_iota(jnp.int32, acc.shape, 1) < heads, g_cum, beta)
    oc_ref[...] = out[:, :2 * heads]
    or_ref[...] = out.T[:2 * heads, :]


def _dn_ab(x2d, g, w_ab, a_log, dt_bias, *, tm):
    m, d = x2d.shape
    heads = a_log.shape[0]
    pad = LANES - 2 * heads
    w_pad = jnp.pad(w_ab, ((0, 0), (0, pad)))
    w_hi = w_pad.astype(BF16)
    w_lo = (w_pad - w_hi.astype(F32)).astype(BF16)
    w3 = jnp.concatenate([w_hi, w_hi, w_lo], axis=0)
    params = jnp.stack([jnp.pad(a_log, (0, LANES - heads)), jnp.pad(dt_bias, (0, LANES - heads))])
    return pl.pallas_call(
        functools.partial(_ab_kernel, heads=heads, chunk=DN_CHUNK),
        grid=(m // tm,),
        in_specs=[pl.BlockSpec((tm, d), lambda i: (i, 0)),
                  pl.BlockSpec((1, d), lambda i: (0, 0)),
                  pl.BlockSpec((3 * d, LANES), lambda i: (0, 0)),
                  pl.BlockSpec((2, LANES), lambda i: (0, 0))],
        out_specs=[pl.BlockSpec((tm, 2 * heads), lambda i: (i, 0)),
                   pl.BlockSpec((2 * heads, tm), lambda i: (0, i))],
        out_shape=[jax.ShapeDtypeStruct((m, 2 * heads), F32), jax.ShapeDtypeStruct((2 * heads, m), F32)],
        compiler_params=_cparams("arbitrary"),
        name="dn_ab",
    )(x2d, g.reshape(1, d), w3, params)


def _split(a):
    hi = a.astype(BF16)
    return hi, (a - hi.astype(F32)).astype(BF16)


def _mm(a, b, dims, passes):
    dg = functools.partial(lax.dot_general, dimension_numbers=dims, preferred_element_type=F32)
    if passes == 6:
        return dg(a, b, precision=HIGHEST)
    if passes == 1:
        return dg(a.astype(BF16), b.astype(BF16))
    ah, al = _split(a)
    bh, bl = _split(b)
    return dg(ah, bh) + (dg(ah, bl) + dg(al, bh))


_NN = (((1,), (0,)), ((), ()))


def _dn_prep_kernel(q_ref, k_ref, v_ref, gbc_ref, gbr_ref,
                    u_ref, w_ref, qd_ref, kd_ref, intra_ref, dec_ref, *, heads, chunk, passes):
    h = pl.program_id(1)
    tc = q_ref.shape[1]
    units = range(tc // chunk)
    gbc = gbc_ref[0]
    lane = lax.broadcasted_iota(jnp.int32, gbc.shape, 1)
    g_col = jnp.sum(jnp.where(lane == h, gbc, 0.0), axis=-1, keepdims=True)
    beta_col = jnp.sum(jnp.where(lane == h + heads, gbc, 0.0), axis=-1, keepdims=True)
    g_row = gbr_ref[pl.ds(h, 1), :]

    ri = lax.broadcasted_iota(jnp.int32, (chunk, chunk), 0)
    ci = lax.broadcasted_iota(jnp.int32, (chunk, chunk), 1)
    causal = ci <= ri
    strict = ci < ri
    eye = jnp.where(ci == ri, 1.0, 0.0).astype(F32)
    rows = [slice(n * chunk, (n + 1) * chunk) for n in units]

    ks = [k_ref[0, rs, :] for rs in rows]
    qs = [q_ref[0, rs, :] for rs in rows]
    gcs = [g_col[rs] for rs in rows]
    bcs = [beta_col[rs] for rs in rows]
    kbs = [ks[n] * bcs[n] for n in units]
    egs = [jnp.exp(gcs[n]) for n in units]
    decays = [jnp.where(causal, jnp.exp(jnp.where(causal, gcs[n] - g_row[:, rows[n]], 0.0)), 0.0) for n in units]
    kks = [_mm(kbs[n], ks[n], _NT, passes) for n in units]
    qks = [_mm(qs[n], ks[n], _NT, 1) for n in units]
    for n in units:
        intra_ref[0, 0, rows[n], :] = jnp.where(causal, qks[n] * decays[n], 0.0).astype(intra_ref.dtype)
        qd_ref[0, rows[n], :] = (qs[n] * egs[n]).astype(qd_ref.dtype)
        g_last = gcs[n][chunk - 1:chunk, :]
        kd_ref[0, rows[n], :] = (ks[n] * jnp.exp(g_last - gcs[n])).astype(kd_ref.dtype)
        grp = dec_ref.shape[3]
        dec_ref[0, 0, n // grp, n % grp:n % grp + 1, :] = jnp.broadcast_to(jnp.exp(g_last), (1, dec_ref.shape[-1]))

    ls = [jnp.where(strict, kks[n] * decays[n], 0.0) for n in units]

    def quarter(b):
        return ((ri // (2 * b)) == (ci // (2 * b))) & ((ri % (2 * b)) >= b) & ((ci % (2 * b)) < b)

    tinvs = [eye - jnp.where(quarter(1), ls[n], 0.0) for n in units]
    b = 2
    while b < chunk:
        cs = [jnp.where(quarter(b), ls[n], 0.0) for n in units]
        tcs = [_mm(tinvs[n], cs[n], _NN, passes) for n in units]
        tinvs = [tinvs[n] - _mm(tcs[n], tinvs[n], _NN, passes) for n in units]
        b *= 2
    for n in units:
        u_ref[0, rows[n], :] = _mm(tinvs[n], v_ref[0, rows[n], :] * bcs[n], _NN, passes).astype(u_ref.dtype)
    for n in units:
        w_ref[0, rows[n], :] = _mm(tinvs[n], kbs[n] * egs[n], _NN, passes).astype(w_ref.dtype)


def _dn_prep(q, k, v, ab_c, ab_r, *, tc, dec_group, passes):
    b, t, qw = q.shape
    heads = qw // DN_DK
    nt = t // tc
    grp = dec_group
    ngrp = tc // (DN_CHUNK * grp)
    qk_spec = pl.BlockSpec((1, tc, DN_DK), lambda bi, hi, ti: (bi, ti, hi))
    v_spec = pl.BlockSpec((1, tc, DN_DV), lambda bi, hi, ti: (bi, ti, hi))
    return pl.pallas_call(
        functools.partial(_dn_prep_kernel, heads=heads, chunk=DN_CHUNK, passes=passes),
        grid=(b, heads, nt),
        in_specs=[qk_spec, qk_spec, v_spec,
                  pl.BlockSpec((1, tc, 2 * heads), lambda bi, hi, ti: (bi, ti, 0)),
                  pl.BlockSpec((2 * heads, tc), lambda bi, hi, ti: (0, bi * nt + ti))],
        out_specs=[v_spec, qk_spec, qk_spec, qk_spec,
                   pl.BlockSpec((1, 1, tc, DN_CHUNK), lambda bi, hi, ti: (bi, hi, ti, 0)),
                   pl.BlockSpec((1, 1, ngrp, grp, DN_DV), lambda bi, hi, ti: (bi, hi, ti, 0, 0))],
        out_shape=[jax.ShapeDtypeStruct((b, t, heads * DN_DV), BF16),
                   jax.ShapeDtypeStruct((b, t, qw), BF16),
                   jax.ShapeDtypeStruct((b, t, qw), BF16),
                   jax.ShapeDtypeStruct((b, t, qw), BF16),
                   jax.ShapeDtypeStruct((b, heads, t, DN_CHUNK), BF16),
                   jax.ShapeDtypeStruct((b, heads, nt * ngrp, grp, DN_DV), F32)],
        compiler_params=_cparams("arbitrary", "arbitrary", "arbitrary"),
        name="dn_prep",
    )(q, k, v, ab_c, ab_r)


def _dn_scan_kernel(u_ref, w_ref, qd_ref, kd_ref, intra_ref, dec_ref, o_ref, s_scr, *, heads, chunk):
    nchunk = u_ref.shape[1] // chunk

    @pl.when(pl.program_id(1) == 0)
    def _():
        s_scr[...] = jnp.zeros(s_scr.shape, F32)

    def body(n, carry):
        rows = pl.ds(pl.multiple_of(n * chunk, chunk), chunk)
        kcols = [slice(h * DN_DK, (h + 1) * DN_DK) for h in range(heads)]
        vcols = [slice(h * DN_DV, (h + 1) * DN_DV) for h in range(heads)]
        ss = [s_scr[h] for h in range(heads)]
        sbs = [s.astype(BF16) for s in ss]
        wq = [jnp.dot(jnp.concatenate([w_ref[0, rows, kcols[h]], qd_ref[0, rows, kcols[h]]], axis=0), sbs[h],
                      preferred_element_type=F32) for h in range(heads)]
        vbs = [(u_ref[0, rows, vcols[h]] - wq[h][:chunk]).astype(BF16) for h in range(heads)]
        for h in range(heads):
            o_ref[0, rows, vcols[h]] = wq[h][chunk:] + jnp.dot(intra_ref[0, h, rows, :], vbs[h],
                                                               preferred_element_type=F32)
        for h in range(heads):
            s_scr[h] = (ss[h] * dec_ref[0, h, 0, pl.ds(n, 1), :]
                        + lax.dot_general(kd_ref[0, rows, kcols[h]], vbs[h], _TN, preferred_element_type=F32))
        return carry

    lax.fori_loop(0, nchunk, body, 0)


def _dn_scan(u, w, qd, kd, intra, dec, *, tc):
    b, t, vw = u.shape
    heads = vw // DN_DV
    qw = heads * DN_DK
    nt = t // tc
    nchunk = tc // DN_CHUNK
    assert dec.shape == (b, heads, nt, nchunk, DN_DV)
    return pl.pallas_call(
        functools.partial(_dn_scan_kernel, heads=heads, chunk=DN_CHUNK),
        grid=(b, nt),
        in_specs=[pl.BlockSpec((1, tc, vw), lambda bi, ti: (bi, ti, 0)),
                  pl.BlockSpec((1, tc, qw), lambda bi, ti: (bi, ti, 0)),
                  pl.BlockSpec((1, tc, qw), lambda bi, ti: (bi, ti, 0)),
                  pl.BlockSpec((1, tc, qw), lambda bi, ti: (bi, ti, 0)),
                  pl.BlockSpec((1, heads, tc, DN_CHUNK), lambda bi, ti: (bi, 0, ti, 0)),
                  pl.BlockSpec((1, heads, 1, nchunk, DN_DV), lambda bi, ti: (bi, 0, ti, 0, 0))],
        out_specs=pl.BlockSpec((1, tc, vw), lambda bi, ti: (bi, ti, 0)),
        out_shape=jax.ShapeDtypeStruct((b, t, vw), F32),
        scratch_shapes=[pltpu.VMEM((heads, DN_DK, DN_DV), F32)],
        compiler_params=_cparams("arbitrary", "arbitrary"),
        name="dn_scan",
    )(u, w, qd, kd, intra, dec)


def _dn_out_kernel(o_ref, gate_ref, gn_ref, w_ref, x_ref, out_ref, *, heads):
    acc = x_ref[...]
    for h in range(heads):
        cols = slice(h * DN_DV, (h + 1) * DN_DV)
        o = o_ref[:, cols]
        ms = jnp.mean(o * o, axis=-1, keepdims=True)
        y = o * lax.rsqrt(ms + RMS_EPS) * gn_ref[...] * gate_ref[:, cols].astype(F32)
        acc = acc + jnp.dot(y.astype(BF16), w_ref[cols, :], preferred_element_type=F32)
    out_ref[...] = acc


def _dn_out(o2d, gate, gn, w_out, x2d, *, tm):
    m, vw = o2d.shape
    d = x2d.shape[1]
    heads = vw // DN_DV
    return pl.pallas_call(
        functools.partial(_dn_out_kernel, heads=heads),
        grid=(m // tm,),
        in_specs=[pl.BlockSpec((tm, vw), lambda i: (i, 0)),
                  pl.BlockSpec((tm, vw), lambda i: (i, 0)),
                  pl.BlockSpec((1, DN_DV), lambda i: (0, 0)),
                  pl.BlockSpec((vw, d), lambda i: (0, 0)),
                  pl.BlockSpec((tm, d), lambda i: (i, 0))],
        out_specs=pl.BlockSpec((tm, d), lambda i: (i, 0)),
        out_shape=jax.ShapeDtypeStruct((m, d), F32),
        compiler_params=_cparams("arbitrary"),
        name="dn_out",
    )(o2d, gate, gn.reshape(1, DN_DV), w_out, x2d)


def _deltanet_layer(x, g, w_in, conv_w, a_log, dt_bias, o_norm_g, w_out):
    b, t, d = x.shape
    m = b * t
    x2d = x.reshape(m, d)
    heads = a_log.shape[0]
    qkw = heads * DN_DK
    vw = heads * DN_DV
    tm = min(512, t)
    nq, nv = qkw // IN_CHUNK, vw // IN_CHUNK
    plan = ([("conv_l2", 0, c * IN_CHUNK, DN_DK, DN_DK ** -0.5) for c in range(nq)]
            + [("conv_l2", 1, c * IN_CHUNK, DN_DK, 1.0) for c in range(nq)]
            + [("conv", 2, c * IN_CHUNK, 0, 1.0) for c in range(nv)]
            + [("silu", 3, c * IN_CHUNK, 0, 1.0) for c in range(nv)])
    q, k, v, gate = _in_proj(x2d, g, w_in[:, :2 * qkw + 2 * vw].astype(BF16), plan, (qkw, qkw, vw, vw), tm=tm,
                             seq=t, conv_w=conv_w)
    ab_c, ab_r = _dn_ab(x2d, g, w_in[:, 2 * qkw + 2 * vw:], a_log, dt_bias, tm=tm)
    tc = min(512, t)
    u, w, qd, kd, intra, dec = _dn_prep(q.reshape(b, t, qkw), k.reshape(b, t, qkw), v.reshape(b, t, vw),
                                        ab_c.reshape(b, t, 2 * heads), ab_r, tc=min(DN_PREP_ROWS, t),
                                        dec_group=tc // DN_CHUNK, passes=DN_PREP_PASSES)
    o = _dn_scan(u, w, qd, kd, intra, dec, tc=tc)
    out = _dn_out(o.reshape(m, vw), gate, o_norm_g, w_out.astype(BF16), x2d, tm=tm)
    return out.reshape(b, t, d)


def _sb_kernel(q_ref, k_ref, v_ref, gate_ref, o_ref, zn_s, l1mb_s, wts_s, sum_s, run_s, acc_s, *, dh):
    i = pl.program_id(2)
    tq = q_ref.shape[1]
    tk = tq
    nh = LANES // dh
    nlb = q_ref.shape[2] // LANES
    lane = lax.broadcasted_iota(jnp.int32, (tq, LANES), 1)
    ri = lax.broadcasted_iota(jnp.int32, (tk, tk), 0)
    ci = lax.broadcasted_iota(jnp.int32, (tk, tk), 1)
    later = jnp.where(ri >= ci, 1.0, 0.0).astype(BF16)
    mask = ci < ri

    chains = [(lb, hh) for lb in range(nlb) for hh in range(nh)]
    nc = len(chains)
    qms = []
    for lb, hh in chains:
        q = q_ref[0, :, lb * LANES:(lb + 1) * LANES]
        qms.append(jnp.where((lane >= hh * dh) & (lane < (hh + 1) * dh), q, jnp.zeros_like(q)))

    def kv_block(ref, s):
        start = pl.multiple_of(jnp.maximum(i - s, 0) * tk, tk)
        return [ref[0, pl.ds(start, tk), lb * LANES:(lb + 1) * LANES] for lb in range(nlb)]

    def stage_a(s, slot):
        kjs = kv_block(k_ref, s)
        for c in range(nc):
            zn_s[slot, c] = lax.dot_general(qms[c], kjs[chains[c][0]], _NT, preferred_element_type=F32)

    def stage_b(slot, diag):
        for c in range(nc):
            zn = zn_s[slot, c]
            neg_abs = lax.bitcast_convert_type(lax.bitcast_convert_type(zn, jnp.uint32) | jnp.uint32(SIGN_BIT), F32)
            l1m = jnp.minimum(zn, 0.0) - jnp.log(1.0 + jnp.exp(neg_abs))
            if diag:
                l1m = jnp.where(mask, l1m, 0.0)
            l1mb_s[slot, c] = l1m.astype(BF16)
            sum_s[slot, c] = jnp.broadcast_to(jnp.sum(l1m, axis=-1, keepdims=True), (tq, LANES))

    def stage_c(slot):
        return [jnp.dot(l1mb_s[slot, c], later, preferred_element_type=F32) for c in range(nc)]

    def stage_d(cums, slot, diag):
        for c in range(nc):
            run = run_s[c]
            w = jnp.exp(cums[c] - zn_s[slot, c] + jnp.concatenate([run] * (tk // LANES), axis=1))
            if diag:
                w = jnp.where(mask, w, 0.0)
            wts_s[slot, c] = w.astype(BF16)
            run_s[c] = run + sum_s[slot, c]

    def stage_e(s, slot):
        vjs = kv_block(v_ref, s)
        for c in range(nc):
            acc_s[c] += jnp.dot(wts_s[slot, c], vjs[chains[c][0]], preferred_element_type=F32)

    run_s[...] = jnp.zeros(run_s.shape, F32)
    acc_s[...] = jnp.zeros(acc_s.shape, F32)
    stage_a(0, 0)
    stage_a(1, 1)
    stage_b(0, True)
    cum0 = stage_c(0)
    stage_b(1, False)
    stage_d(cum0, 0, True)
    stage_a(2, 0)

    def trip(t, par):
        cums = stage_c(par)
        stage_e(t - 1, 1 - par)
        stage_b(1 - par, False)
        stage_d(cums, par, False)
        stage_a(t + 2, par)

    def pair(p, carry):
        trip(2 * p + 1, 1)
        trip(2 * p + 2, 0)
        return carry

    lax.fori_loop(0, i // 2, pair, 0)

    @pl.when(i % 2 == 1)
    def _():
        trip(i, 1)

    @pl.when(i % 2 == 1)
    def _():
        stage_e(i, 1)

    @pl.when(i % 2 == 0)
    def _():
        stage_e(i, 0)

    for c in range(0, nc, nh):
        lb = chains[c][0]
        o = acc_s[c]
        for hh in range(1, nh):
            o = jnp.where(lane >= hh * dh, acc_s[c + hh], o)
        cs = slice(lb * LANES, (lb + 1) * LANES)
        o_ref[0, :, cs] = (o * gate_ref[0, :, cs].astype(F32)).astype(o_ref.dtype)


def _sb_attention(q, k, v, gate, *, tq, bw):
    b, t, w = q.shape
    nc = bw // SB_DH
    blk = pl.BlockSpec((1, tq, bw), lambda bi, hi, ti: (bi, ti, hi))
    full = pl.BlockSpec((1, t, bw), lambda bi, hi, ti: (bi, 0, hi))
    return pl.pallas_call(
        functools.partial(_sb_kernel, dh=SB_DH),
        grid=(b, w // bw, t // tq),
        in_specs=[blk, full, full, blk],
        out_specs=blk,
        out_shape=jax.ShapeDtypeStruct((b, t, w), BF16),
        scratch_shapes=[pltpu.VMEM((2, nc, tq, tq), F32),
                        pltpu.VMEM((2, nc, tq, tq), BF16),
                        pltpu.VMEM((2, nc, tq, tq), BF16),
                        pltpu.VMEM((2, nc, tq, LANES), F32),
                        pltpu.VMEM((nc, tq, LANES), F32),
                        pltpu.VMEM((nc, tq, LANES), F32)],
        compiler_params=_cparams("arbitrary", "arbitrary", "arbitrary"),
        name="sb_attn",
    )(q, k, v, gate)


def _stickbreak_layer(x, g, w_in, q_norm_g, k_norm_g, w_out):
    b, t, d = x.shape
    m = b * t
    x2d = x.reshape(m, d)
    w = w_in.shape[1] // 4
    heads = w // SB_DH
    tm = min(512, t)
    gains = jnp.concatenate([jnp.tile(q_norm_g, heads), jnp.tile(k_norm_g, heads)]).reshape(1, 2 * w)
    nw = w // IN_CHUNK
    plan = ([("rms", 0, c * IN_CHUNK, SB_DH, -(SB_DH ** -0.5)) for c in range(nw)]
            + [("rms", 1, c * IN_CHUNK, SB_DH, 1.0) for c in range(nw)]
            + [("plain", 2, c * IN_CHUNK, 0, 1.0) for c in range(nw)]
            + [("silu", 3, c * IN_CHUNK, 0, 1.0) for c in range(nw)])
    q, k, v, gate = _in_proj(x2d, g, w_in.astype(BF16), plan, (w, w, w, w), tm=tm, seq=t, gain=gains)
    o = _sb_attention(q.reshape(b, t, w), k.reshape(b, t, w), v.reshape(b, t, w), gate.reshape(b, t, w),
                      tq=min(256, t), bw=256)
    out = _out_proj(o.reshape(m, w), w_out.astype(BF16), x2d, tm=tm)
    return out.reshape(b, t, d)


def _sc_kernel(x_ref, g_ref, wb_ref, wc_ref, wu_ref, wg_ref, cw_ref, wo_ref, o_ref,
               h_scr, acc_scr, tail_scr, work_scr, *, seq):
    i = pl.program_id(0)
    j = pl.program_id(1)
    tm = x_ref.shape[0]

    @pl.when(j == 0)
    def _():
        _normed_rows(x_ref, g_ref, h_scr)
        acc_scr[...] = x_ref[...]

    h = h_scr[...]
    cu = (jnp.dot(h, wc_ref[...], preferred_element_type=F32) * jnp.dot(h, wu_ref[...], preferred_element_type=F32))

    @pl.when(((i * tm) % seq == 0) & (j == 0))
    def _():
        tail_scr[...] = jnp.zeros(tail_scr.shape, F32)

    y = _causal_conv(cu, cw_ref[...], tail_scr.at[j], work_scr)
    y = y * jnp.dot(h, wb_ref[...], preferred_element_type=F32)
    y = y * _silu(jnp.dot(h, wg_ref[...], preferred_element_type=F32))
    acc_scr[...] += jnp.dot(y.astype(BF16), wo_ref[...], preferred_element_type=F32)

    @pl.when(j == pl.num_programs(1) - 1)
    def _():
        o_ref[...] = acc_scr[...]


def _shortconv_layer(x, g, w_in, conv_w, w_out):
    b, t, d = x.shape
    m = b * t
    x2d = x.reshape(m, d)
    w = w_in.shape[1] // 4
    tm = min(512, t)
    tn = min(512, w)
    nj = w // tn
    w_bf = w_in.astype(BF16)
    taps = conv_w.shape[0]

    def wspec(part):
        return pl.BlockSpec((d, tn), lambda i, j: (0, part * nj + j))

    out = pl.pallas_call(
        functools.partial(_sc_kernel, seq=t),
        grid=(m // tm, nj),
        in_specs=[pl.BlockSpec((tm, d), lambda i, j: (i, 0)),
                  pl.BlockSpec((1, d), lambda i, j: (0, 0)),
                  wspec(0), wspec(1), wspec(2), wspec(3),
                  pl.BlockSpec((taps, tn), lambda i, j: (0, j)),
                  pl.BlockSpec((tn, d), lambda i, j: (j, 0))],
        out_specs=pl.BlockSpec((tm, d), lambda i, j: (i, 0)),
        out_shape=jax.ShapeDtypeStruct((m, d), F32),
        scratch_shapes=[pltpu.VMEM((tm, d), BF16), pltpu.VMEM((tm, d), F32),
                        pltpu.VMEM((nj, SUBLANES, tn), F32), pltpu.VMEM((SUBLANES + tm, tn), F32)],
        compiler_params=_cparams("arbitrary", "arbitrary"),
        name="shortconv_layer",
    )(x2d, g.reshape(1, d), w_bf, w_bf, w_bf, w_bf, conv_w, w_out.astype(BF16))
    return out.reshape(b, t, d)


def kernel(x, norm_g, dn_w_in, dn_conv_w, dn_a_log, dn_dt_bias, dn_o_norm_g, dn_w_out, sb_w_in, sb_q_norm_g,
           sb_k_norm_g, sb_w_out, sc_w_in, sc_conv_w, sc_w_out):
    depth = norm_g.shape[0]
    n_mixers = 3
    for i in range(depth):
        j = i // n_mixers
        kind = i % n_mixers
        if kind == 0:
            x = _deltanet_layer(x, norm_g[i], dn_w_in[j], dn_conv_w[j], dn_a_log[j], dn_dt_bias[j],
                                dn_o_norm_g[j], dn_w_out[j])
        elif kind == 1:
            x = _stickbreak_layer(x, norm_g[i], sb_w_in[j], sb_q_norm_g[j], sb_k_norm_g[j], sb_w_out[j])
        else:
            x = _shortconv_layer(x, norm_g[i], sc_w_in[j], sc_conv_w[j], sc_w_out[j])
    return x
```

```python
import functools

import jax
import jax.numpy as jnp
from jax import lax
from jax.experimental import pallas as pl
from jax.experimental.pallas import tpu as pltpu

F32 = jnp.float32
BF16 = jnp.bfloat16
HIGHEST = lax.Precision.HIGHEST

SIGN_BIT = 0x80000000
RMS_EPS = 1e-6
L2_EPS = 1e-6
DN_HEADS = 8
DN_DK = 128
DN_DV = 256
DN_CHUNK = 64
DN_PREP_ROWS = 2048
DN_PREP_PASSES = 1
SB_DH = 64
LANES = 128
SUBLANES = 8
VMEM_LIMIT = 56 * 1024 * 1024

_NT = (((1,), (1,)), ((), ()))
_TN = (((0,), (0,)), ((), ()))


def _cparams(*sem):
    return pltpu.CompilerParams(dimension_semantics=sem, vmem_limit_bytes=VMEM_LIMIT)


def _silu(y):
    half = 0.5 * y
    return half + half * jnp.tanh(half)


def _softplus(y):
    return jnp.maximum(y, 0.0) + jnp.log1p(jnp.exp(-jnp.abs(y)))


def _normed_rows(x_ref, g_ref, h_scr):
    x = x_ref[...]
    ms = jnp.mean(x * x, axis=-1, keepdims=True)
    h_scr[...] = (x * lax.rsqrt(ms + RMS_EPS) * g_ref[...]).astype(h_scr.dtype)


def _causal_conv(acc, cw, tail_ref, work_ref):
    tm = acc.shape[0]
    taps = cw.shape[0]
    work_ref[0:SUBLANES, :] = tail_ref[...]
    work_ref[SUBLANES:SUBLANES + tm, :] = acc
    tail_ref[...] = acc[tm - SUBLANES:tm, :]
    y = acc * cw[taps - 1:taps, :]
    for s in range(1, taps):
        y = y + work_ref[SUBLANES - s:SUBLANES - s + tm, :] * cw[taps - 1 - s:taps - s, :]
    return y


def _group_sum_sq(y, group):
    width = 2 * LANES
    r = lax.broadcasted_iota(jnp.int32, (width, width), 0) // group
    c = lax.broadcasted_iota(jnp.int32, (width, width), 1) // group
    ones = jnp.where(r == c, 1.0, 0.0).astype(BF16)
    sq = (y * y).astype(BF16)
    parts = [jnp.dot(sq[:, s:s + width], ones, preferred_element_type=F32) for s in range(0, y.shape[1], width)]
    return jnp.concatenate(parts, axis=-1)


IN_CHUNK = 512


def _in_proj_kernel(*refs, plan, seq, n_out, n_conv, has_gain):
    refs = list(refs)
    x_ref, g_ref, w_ref = refs[:3]
    pos = 3
    cw_ref = gn_ref = tail_scr = work_scr = None
    if n_conv:
        cw_ref = refs[pos]
        pos += 1
    if has_gain:
        gn_ref = refs[pos]
        pos += 1
    out_refs = refs[pos:pos + n_out]
    if n_conv:
        tail_scr, work_scr = refs[pos + n_out:pos + n_out + 2]
    tm = x_ref.shape[0]
    x = x_ref[...]
    ms = jnp.mean(x * x, axis=-1, keepdims=True)
    h = (x * lax.rsqrt(ms + RMS_EPS) * g_ref[...]).astype(BF16)

    if n_conv:
        @pl.when((pl.program_id(0) * tm) % seq == 0)
        def _():
            tail_scr[...] = jnp.zeros(tail_scr.shape, F32)

    for c, (mode, out, off, group, scale) in enumerate(plan):
        cols = slice(c * IN_CHUNK, (c + 1) * IN_CHUNK)
        acc = jnp.dot(h, w_ref[:, cols], preferred_element_type=F32)
        if mode == "plain":
            y = acc
        elif mode == "silu":
            y = _silu(acc)
        elif mode in ("conv", "conv_l2"):
            y = _silu(_causal_conv(acc, cw_ref[:, cols], tail_scr.at[c], work_scr.at[c]))
            if mode == "conv_l2":
                y = y * (lax.rsqrt(_group_sum_sq(y, group) + L2_EPS) * scale)
        elif mode == "rms":
            y = acc * lax.rsqrt(_group_sum_sq(acc, group) * (1.0 / group) + RMS_EPS) * (gn_ref[:, cols] * scale)
        out_refs[out][:, off:off + IN_CHUNK] = y.astype(out_refs[out].dtype)


def _in_proj(x2d, g, w, plan, out_widths, *, tm, seq, conv_w=None, gain=None):
    m, d = x2d.shape
    n = w.shape[1]
    assert m % tm == 0 and seq % tm == 0 and n == len(plan) * IN_CHUNK
    n_conv = sum(1 for p in plan if p[0].startswith("conv"))
    assert all(p[0].startswith("conv") for p in plan[:n_conv])
    const = lambda i: (0, 0)
    in_specs = [pl.BlockSpec((tm, d), lambda i: (i, 0)), pl.BlockSpec((1, d), const), pl.BlockSpec((d, n), const)]
    args = [x2d, g.reshape(1, d), w]
    scratch = []
    if n_conv:
        in_specs.append(pl.BlockSpec(conv_w.shape, const))
        args.append(conv_w)
        scratch = [pltpu.VMEM((n_conv, SUBLANES, IN_CHUNK), F32), pltpu.VMEM((n_conv, SUBLANES + tm, IN_CHUNK), F32)]
    if gain is not None:
        in_specs.append(pl.BlockSpec(gain.shape, const))
        args.append(gain)
    return pl.pallas_call(
        functools.partial(_in_proj_kernel, plan=tuple(plan), seq=seq, n_out=len(out_widths), n_conv=n_conv,
                          has_gain=gain is not None),
        grid=(m // tm,),
        in_specs=in_specs,
        out_specs=[pl.BlockSpec((tm, ow), lambda i: (i, 0)) for ow in out_widths],
        out_shape=[jax.ShapeDtypeStruct((m, ow), BF16) for ow in out_widths],
        scratch_shapes=scratch,
        compiler_params=_cparams("arbitrary"),
        name="in_proj",
    )(*args)


def _out_kernel(a_ref, w_ref, x_ref, o_ref):
    o_ref[...] = x_ref[...] + jnp.dot(a_ref[...], w_ref[...], preferred_element_type=F32)


def _out_proj(a, w, x2d, *, tm):
    m, k = a.shape
    d = w.shape[1]
    return pl.pallas_call(
        _out_kernel,
        grid=(m // tm,),
        in_specs=[pl.BlockSpec((tm, k), lambda i: (i, 0)),
                  pl.BlockSpec((k, d), lambda i: (0, 0)),
                  pl.BlockSpec((tm, d), lambda i: (i, 0))],
        out_specs=pl.BlockSpec((tm, d), lambda i: (i, 0)),
        out_shape=jax.ShapeDtypeStruct((m, d), F32),
        compiler_params=_cparams("arbitrary"),
        name="out_proj",
    )(a, w, x2d)


def _ab_kernel(x_ref, g_ref, w3_ref, p_ref, oc_ref, or_ref, *, heads, chunk):
    x = x_ref[...]
    tm = x.shape[0]
    ms = jnp.mean(x * x, axis=-1, keepdims=True)
    hh, hl = _split(x * lax.rsqrt(ms + RMS_EPS) * g_ref[...])
    acc = jnp.dot(jnp.concatenate([hh, hl, hh], axis=1), w3_ref[...], preferred_element_type=F32)
    la = -jnp.exp(p_ref[0:1, :]) * _softplus(acc + p_ref[1:2, :])
    beta = jax.nn.sigmoid(acc)
    r = lax.broadcasted_iota(jnp.int32, (tm, tm), 0)
    c = lax.broadcasted_iota(jnp.int32, (tm, tm), 1)
    tril = jnp.where(((r // chunk) == (c // chunk)) & (c <= r), 1.0, 0.0).astype(BF16)
    lah, lal = _split(la)
    g_cum = jnp.dot(jnp.concatenate([tril, tril], axis=1), jnp.concatenate([lah, lal], axis=0),
                    preferred_element_type=F32)
    out = jnp.where(lax.broadcasted_iota(jnp.int32, acc.shape, 1) < heads, g_cum, beta)
    oc_ref[...] = out[:, :2 * heads]
    or_ref[...] = out.T[:2 * heads, :]


def _dn_ab(x2d, g, w_ab, a_log, dt_bias, *, tm):
    m, d = x2d.shape
    heads = a_log.shape[0]
    pad = LANES - 2 * heads
    w_pad = jnp.pad(w_ab, ((0, 0), (0, pad)))
    w_hi = w_pad.astype(BF16)
    w_lo = (w_pad - w_hi.astype(F32)).astype(BF16)
    w3 = jnp.concatenate([w_hi, w_hi, w_lo], axis=0)
    params = jnp.stack([jnp.pad(a_log, (0, LANES - heads)), jnp.pad(dt_bias, (0, LANES - heads))])
    return pl.pallas_call(
        functools.partial(_ab_kernel, heads=heads, chunk=DN_CHUNK),
        grid=(m // tm,),
        in_specs=[pl.BlockSpec((tm, d), lambda i: (i, 0)),
                  pl.BlockSpec((1, d), lambda i: (0, 0)),
                  pl.BlockSpec((3 * d, LANES), lambda i: (0, 0)),
                  pl.BlockSpec((2, LANES), lambda i: (0, 0))],
        out_specs=[pl.BlockSpec((tm, 2 * heads), lambda i: (i, 0)),
                   pl.BlockSpec((2 * heads, tm), lambda i: (0, i))],
        out_shape=[jax.ShapeDtypeStruct((m, 2 * heads), F32), jax.ShapeDtypeStruct((2 * heads, m), F32)],
        compiler_params=_cparams("arbitrary"),
        name="dn_ab",
    )(x2d, g.reshape(1, d), w3, params)


def _split(a):
    hi = a.astype(BF16)
    return hi, (a - hi.astype(F32)).astype(BF16)


def _mm(a, b, dims, passes):
    dg = functools.partial(lax.dot_general, dimension_numbers=dims, preferred_element_type=F32)
    if passes == 6:
        return dg(a, b, precision=HIGHEST)
    if passes == 1:
        return dg(a.astype(BF16), b.astype(BF16))
    ah, al = _split(a)
    bh, bl = _split(b)
    return dg(ah, bh) + (dg(ah, bl) + dg(al, bh))


_NN = (((1,), (0,)), ((), ()))


def _dn_prep_kernel(q_ref, k_ref, v_ref, gbc_ref, gbr_ref,
                    u_ref, w_ref, qd_ref, kd_ref, intra_ref, dec_ref, *, heads, chunk, passes):
    h = pl.program_id(1)
    tc = q_ref.shape[1]
    units = range(tc // chunk)
    gbc = gbc_ref[0]
    lane = lax.broadcasted_iota(jnp.int32, gbc.shape, 1)
    g_col = jnp.sum(jnp.where(lane == h, gbc, 0.0), axis=-1, keepdims=True)
    beta_col = jnp.sum(jnp.where(lane == h + heads, gbc, 0.0), axis=-1, keepdims=True)
    g_row = gbr_ref[pl.ds(h, 1), :]

    ri = lax.broadcasted_iota(jnp.int32, (chunk, chunk), 0)
    ci = lax.broadcasted_iota(jnp.int32, (chunk, chunk), 1)
    causal = ci <= ri
    strict = ci < ri
    eye = jnp.where(ci == ri, 1.0, 0.0).astype(F32)
    rows = [slice(n * chunk, (n + 1) * chunk) for n in units]

    ks = [k_ref[0, rs, :] for rs in rows]
    qs = [q_ref[0, rs, :] for rs in rows]
    gcs = [g_col[rs] for rs in rows]
    bcs = [beta_col[rs] for rs in rows]
    kbs = [ks[n] * bcs[n] for n in units]
    egs = [jnp.exp(gcs[n]) for n in units]
    decays = [jnp.where(causal, jnp.exp(jnp.where(causal, gcs[n] - g_row[:, rows[n]], 0.0)), 0.0) for n in units]
    kks = [_mm(kbs[n], ks[n], _NT, passes) for n in units]
    qks = [_mm(qs[n], ks[n], _NT, 1) for n in units]
    for n in units:
        intra_ref[0, 0, rows[n], :] = jnp.where(causal, qks[n] * decays[n], 0.0).astype(intra_ref.dtype)
        qd_ref[0, rows[n], :] = (qs[n] * egs[n]).astype(qd_ref.dtype)
        g_last = gcs[n][chunk - 1:chunk, :]
        kd_ref[0, rows[n], :] = (ks[n] * jnp.exp(g_last - gcs[n])).astype(kd_ref.dtype)
        grp = dec_ref.shape[3]
        dec_ref[0, 0, n // grp, n % grp:n % grp + 1, :] = jnp.broadcast_to(jnp.exp(g_last), (1, dec_ref.shape[-1]))

    ls = [jnp.where(strict, kks[n] * decays[n], 0.0) for n in units]

    def quarter(b):
        return ((ri // (2 * b)) == (ci // (2 * b))) & ((ri % (2 * b)) >= b) & ((ci % (2 * b)) < b)

    tinvs = [eye - jnp.where(quarter(1), ls[n], 0.0) for n in units]
    b = 2
    while b < chunk:
        cs = [jnp.where(quarter(b), ls[n], 0.0) for n in units]
        tcs = [_mm(tinvs[n], cs[n], _NN, passes) for n in units]
        tinvs = [tinvs[n] - _mm(tcs[n], tinvs[n], _NN, passes) for n in units]
        b *= 2
    for n in units:
        u_ref[0, rows[n], :] = _mm(tinvs[n], v_ref[0, rows[n], :] * bcs[n], _NN, passes).astype(u_ref.dtype)
    for n in units:
        w_ref[0, rows[n], :] = _mm(tinvs[n], kbs[n] * egs[n], _NN, passes).astype(w_ref.dtype)


def _dn_prep(q, k, v, ab_c, ab_r, *, tc, dec_group, passes):
    b, t, qw = q.shape
    heads = qw // DN_DK
    nt = t // tc
    grp = dec_group
    ngrp = tc // (DN_CHUNK * grp)
    qk_spec = pl.BlockSpec((1, tc, DN_DK), lambda bi, hi, ti: (bi, ti, hi))
    v_spec = pl.BlockSpec((1, tc, DN_DV), lambda bi, hi, ti: (bi, ti, hi))
    return pl.pallas_call(
        functools.partial(_dn_prep_kernel, heads=heads, chunk=DN_CHUNK, passes=passes),
        grid=(b, heads, nt),
        in_specs=[qk_spec, qk_spec, v_spec,
                  pl.BlockSpec((1, tc, 2 * heads), lambda bi, hi, ti: (bi, ti, 0)),
                  pl.BlockSpec((2 * heads, tc), lambda bi, hi, ti: (0, bi * nt + ti))],
        out_specs=[v_spec, qk_spec, qk_spec, qk_spec,
                   pl.BlockSpec((1, 1, tc, DN_CHUNK), lambda bi, hi, ti: (bi, hi, ti, 0)),
                   pl.BlockSpec((1, 1, ngrp, grp, DN_DV), lambda bi, hi, ti: (bi, hi, ti, 0, 0))],
        out_shape=[jax.ShapeDtypeStruct((b, t, heads * DN_DV), BF16),
                   jax.ShapeDtypeStruct((b, t, qw), BF16),
                   jax.ShapeDtypeStruct((b, t, qw), BF16),
                   jax.ShapeDtypeStruct((b, t, qw), BF16),
                   jax.ShapeDtypeStruct((b, heads, t, DN_CHUNK), BF16),
                   jax.ShapeDtypeStruct((b, heads, nt * ngrp, grp, DN_DV), F32)],
        compiler_params=_cparams("arbitrary", "arbitrary", "arbitrary"),
        name="dn_prep",
    )(q, k, v, ab_c, ab_r)


def _dn_scan_kernel(u_ref, w_ref, qd_ref, kd_ref, intra_ref, dec_ref, o_ref, s_scr, *, heads, chunk):
    nchunk = u_ref.shape[1] // chunk

    @pl.when(pl.program_id(1) == 0)
    def _():
        s_scr[...] = jnp.zeros(s_scr.shape, F32)

    def body(n, carry):
        rows = pl.ds(pl.multiple_of(n * chunk, chunk), chunk)
        kcols = [slice(h * DN_DK, (h + 1) * DN_DK) for h in range(heads)]
        vcols = [slice(h * DN_DV, (h + 1) * DN_DV) for h in range(heads)]
        ss = [s_scr[h] for h in range(heads)]
        sbs = [s.astype(BF16) for s in ss]
        wq = [jnp.dot(jnp.concatenate([w_ref[0, rows, kcols[h]], qd_ref[0, rows, kcols[h]]], axis=0), sbs[h],
                      preferred_element_type=F32) for h in range(heads)]
        vbs = [(u_ref[0, rows, vcols[h]] - wq[h][:chunk]).astype(BF16) for h in range(heads)]
        for h in range(heads):
            o_ref[0, rows, vcols[h]] = wq[h][chunk:] + jnp.dot(intra_ref[0, h, rows, :], vbs[h],
                                                               preferred_element_type=F32)
        for h in range(heads):
            s_scr[h] = (ss[h] * dec_ref[0, h, 0, pl.ds(n, 1), :]
                        + lax.dot_general(kd_ref[0, rows, kcols[h]], vbs[h], _TN, preferred_element_type=F32))
        return carry

    lax.fori_loop(0, nchunk, body, 0)


def _dn_scan(u, w, qd, kd, intra, dec, *, tc):
    b, t, vw = u.shape
    heads = vw // DN_DV
    qw = heads * DN_DK
    nt = t // tc
    nchunk = tc // DN_CHUNK
    assert dec.shape == (b, heads, nt, nchunk, DN_DV)
    return pl.pallas_call(
        functools.partial(_dn_scan_kernel, heads=heads, chunk=DN_CHUNK),
        grid=(b, nt),
        in_specs=[pl.BlockSpec((1, tc, vw), lambda bi, ti: (bi, ti, 0)),
                  pl.BlockSpec((1, tc, qw), lambda bi, ti: (bi, ti, 0)),
                  pl.BlockSpec((1, tc, qw), lambda bi, ti: (bi, ti, 0)),
                  pl.BlockSpec((1, tc, qw), lambda bi, ti: (bi, ti, 0)),
                  pl.BlockSpec((1, heads, tc, DN_CHUNK), lambda bi, ti: (bi, 0, ti, 0)),
                  pl.BlockSpec((1, heads, 1, nchunk, DN_DV), lambda bi, ti: (bi, 0, ti, 0, 0))],
        out_specs=pl.BlockSpec((1, tc, vw), lambda bi, ti: (bi, ti, 0)),
        out_shape=jax.ShapeDtypeStruct((b, t, vw), F32),
        scratch_shapes=[pltpu.VMEM((heads, DN_DK, DN_DV), F32)],
        compiler_params=_cparams("arbitrary", "arbitrary"),
        name="dn_scan",
    )(u, w, qd, kd, intra, dec)


def _dn_out_kernel(o_ref, gate_ref, gn_ref, w_ref, x_ref, out_ref, *, heads):
    parts = []
    for h in range(heads):
        cols = slice(h * DN_DV, (h + 1) * DN_DV)
        o = o_ref[:, cols]
        ms = jnp.mean(o * o, axis=-1, keepdims=True)
        parts.append((o * lax.rsqrt(ms + RMS_EPS) * gn_ref[...] * gate_ref[:, cols].astype(F32)).astype(BF16))
    out_ref[...] = x_ref[...] + jnp.dot(jnp.concatenate(parts, axis=1), w_ref[...], preferred_element_type=F32)


def _dn_out(o2d, gate, gn, w_out, x2d, *, tm):
    m, vw = o2d.shape
    d = x2d.shape[1]
    heads = vw // DN_DV
    return pl.pallas_call(
        functools.partial(_dn_out_kernel, heads=heads),
        grid=(m // tm,),
        in_specs=[pl.BlockSpec((tm, vw), lambda i: (i, 0)),
                  pl.BlockSpec((tm, vw), lambda i: (i, 0)),
                  pl.BlockSpec((1, DN_DV), lambda i: (0, 0)),
                  pl.BlockSpec((vw, d), lambda i: (0, 0)),
                  pl.BlockSpec((tm, d), lambda i: (i, 0))],
        out_specs=pl.BlockSpec((tm, d), lambda i: (i, 0)),
        out_shape=jax.ShapeDtypeStruct((m, d), F32),
        compiler_params=_cparams("arbitrary"),
        name="dn_out",
    )(o2d, gate, gn.reshape(1, DN_DV), w_out, x2d)


def _deltanet_layer(x, g, w_in, conv_w, a_log, dt_bias, o_norm_g, w_out):
    b, t, d = x.shape
    m = b * t
    x2d = x.reshape(m, d)
    heads = a_log.shape[0]
    qkw = heads * DN_DK
    vw = heads * DN_DV
    tm = min(512, t)
    nq, nv = qkw // IN_CHUNK, vw // IN_CHUNK
    plan = ([("conv_l2", 0, c * IN_CHUNK, DN_DK, DN_DK ** -0.5) for c in range(nq)]
            + [("conv_l2", 1, c * IN_CHUNK, DN_DK, 1.0) for c in range(nq)]
            + [("conv", 2, c * IN_CHUNK, 0, 1.0) for c in range(nv)]
            + [("silu", 3, c * IN_CHUNK, 0, 1.0) for c in range(nv)])
    q, k, v, gate = _in_proj(x2d, g, w_in[:, :2 * qkw + 2 * vw].astype(BF16), plan, (qkw, qkw, vw, vw), tm=tm,
                             seq=t, conv_w=conv_w)
    ab_c, ab_r = _dn_ab(x2d, g, w_in[:, 2 * qkw + 2 * vw:], a_log, dt_bias, tm=tm)
    tc = min(512, t)
    u, w, qd, kd, intra, dec = _dn_prep(q.reshape(b, t, qkw), k.reshape(b, t, qkw), v.reshape(b, t, vw),
                                        ab_c.reshape(b, t, 2 * heads), ab_r, tc=min(DN_PREP_ROWS, t),
                                        dec_group=tc // DN_CHUNK, passes=DN_PREP_PASSES)
    o = _dn_scan(u, w, qd, kd, intra, dec, tc=tc)
    out = _dn_out(o.reshape(m, vw), gate, o_norm_g, w_out.astype(BF16), x2d, tm=tm)
    return out.reshape(b, t, d)


def _sb_kernel(q_ref, k_ref, v_ref, gate_ref, o_ref, zn_s, l1mb_s, wts_s, sum_s, run_s, acc_s, *, dh):
    i = pl.program_id(2)
    tq = q_ref.shape[1]
    tk = tq
    nh = LANES // dh
    nlb = q_ref.shape[2] // LANES
    lane = lax.broadcasted_iota(jnp.int32, (tq, LANES), 1)
    ri = lax.broadcasted_iota(jnp.int32, (tk, tk), 0)
    ci = lax.broadcasted_iota(jnp.int32, (tk, tk), 1)
    later = jnp.where(ri >= ci, 1.0, 0.0).astype(BF16)
    mask = ci < ri

    chains = [(lb, hh) for lb in range(nlb) for hh in range(nh)]
    nc = len(chains)
    qms = []
    for lb, hh in chains:
        q = q_ref[0, :, lb * LANES:(lb + 1) * LANES]
        qms.append(jnp.where((lane >= hh * dh) & (lane < (hh + 1) * dh), q, jnp.zeros_like(q)))

    def kv_block(ref, s):
        start = pl.multiple_of(jnp.maximum(i - s, 0) * tk, tk)
        return [ref[0, pl.ds(start, tk), lb * LANES:(lb + 1) * LANES] for lb in range(nlb)]

    def stage_a(s, slot):
        kjs = kv_block(k_ref, s)
        for c in range(nc):
            zn_s[slot, c] = lax.dot_general(qms[c], kjs[chains[c][0]], _NT, preferred_element_type=F32)

    def stage_b(slot, diag):
        for c in range(nc):
            zn = zn_s[slot, c]
            neg_abs = lax.bitcast_convert_type(lax.bitcast_convert_type(zn, jnp.uint32) | jnp.uint32(SIGN_BIT), F32)
            l1m = jnp.minimum(zn, 0.0) - jnp.log(1.0 + jnp.exp(neg_abs))
            if diag:
                l1m = jnp.where(mask, l1m, 0.0)
            l1mb_s[slot, c] = l1m.astype(BF16)
            sum_s[slot, c] = jnp.broadcast_to(jnp.sum(l1m, axis=-1, keepdims=True), (tq, LANES))

    def stage_c(slot):
        return [jnp.dot(l1mb_s[slot, c], later, preferred_element_type=F32) for c in range(nc)]

    def stage_d(cums, slot, diag):
        for c in range(nc):
            run = run_s[c]
            w = jnp.exp(cums[c] - zn_s[slot, c] + jnp.concatenate([run] * (tk // LANES), axis=1))
            if diag:
                w = jnp.where(mask, w, 0.0)
            wts_s[slot, c] = w.astype(BF16)
            run_s[c] = run + sum_s[slot, c]

    def stage_e(s, slot):
        vjs = kv_block(v_ref, s)
        for c in range(nc):
            acc_s[c] += jnp.dot(wts_s[slot, c], vjs[chains[c][0]], preferred_element_type=F32)

    run_s[...] = jnp.zeros(run_s.shape, F32)
    acc_s[...] = jnp.zeros(acc_s.shape, F32)
    stage_a(0, 0)
    stage_a(1, 1)
    stage_b(0, True)
    cum0 = stage_c(0)
    stage_b(1, False)
    stage_d(cum0, 0, True)
    stage_a(2, 0)

    def trip(t, par):
        cums = stage_c(par)
        stage_e(t - 1, 1 - par)
        stage_b(1 - par, False)
        stage_d(cums, par, False)
        stage_a(t + 2, par)

    def pair(p, carry):
        trip(2 * p + 1, 1)
        trip(2 * p + 2, 0)
        return carry

    lax.fori_loop(0, i // 2, pair, 0)

    @pl.when(i % 2 == 1)
    def _():
        trip(i, 1)

    @pl.when(i % 2 == 1)
    def _():
        stage_e(i, 1)

    @pl.when(i % 2 == 0)
    def _():
        stage_e(i, 0)

    for c in range(0, nc, nh):
        lb = chains[c][0]
        o = acc_s[c]
        for hh in range(1, nh):
            o = jnp.where(lane >= hh * dh, acc_s[c + hh], o)
        cs = slice(lb * LANES, (lb + 1) * LANES)
        o_ref[0, :, cs] = (o * gate_ref[0, :, cs].astype(F32)).astype(o_ref.dtype)


def _sb_attention(q, k, v, gate, *, tq, bw):
    b, t, w = q.shape
    nc = bw // SB_DH
    blk = pl.BlockSpec((1, tq, bw), lambda bi, hi, ti: (bi, ti, hi))
    full = pl.BlockSpec((1, t, bw), lambda bi, hi, ti: (bi, 0, hi))
    return pl.pallas_call(
        functools.partial(_sb_kernel, dh=SB_DH),
        grid=(b, w // bw, t // tq),
        in_specs=[blk, full, full, blk],
        out_specs=blk,
        out_shape=jax.ShapeDtypeStruct((b, t, w), BF16),
        scratch_shapes=[pltpu.VMEM((2, nc, tq, tq), F32),
                        pltpu.VMEM((2, nc, tq, tq), BF16),
                        pltpu.VMEM((2, nc, tq, tq), BF16),
                        pltpu.VMEM((2, nc, tq, LANES), F32),
                        pltpu.VMEM((nc, tq, LANES), F32),
                        pltpu.VMEM((nc, tq, LANES), F32)],
        compiler_params=_cparams("arbitrary", "arbitrary", "arbitrary"),
        name="sb_attn",
    )(q, k, v, gate)


def _stickbreak_layer(x, g, w_in, q_norm_g, k_norm_g, w_out):
    b, t, d = x.shape
    m = b * t
    x2d = x.reshape(m, d)
    w = w_in.shape[1] // 4
    heads = w // SB_DH
    tm = min(512, t)
    gains = jnp.concatenate([jnp.tile(q_norm_g, heads), jnp.tile(k_norm_g, heads)]).reshape(1, 2 * w)
    nw = w // IN_CHUNK
    plan = ([("rms", 0, c * IN_CHUNK, SB_DH, -(SB_DH ** -0.5)) for c in range(nw)]
            + [("rms", 1, c * IN_CHUNK, SB_DH, 1.0) for c in range(nw)]
            + [("plain", 2, c * IN_CHUNK, 0, 1.0) for c in range(nw)]
            + [("silu", 3, c * IN_CHUNK, 0, 1.0) for c in range(nw)])
    q, k, v, gate = _in_proj(x2d, g, w_in.astype(BF16), plan, (w, w, w, w), tm=tm, seq=t, gain=gains)
    o = _sb_attention(q.reshape(b, t, w), k.reshape(b, t, w), v.reshape(b, t, w), gate.reshape(b, t, w),
                      tq=min(256, t), bw=256)
    out = _out_proj(o.reshape(m, w), w_out.astype(BF16), x2d, tm=tm)
    return out.reshape(b, t, d)


def _sc_kernel(x_ref, g_ref, wb_ref, wc_ref, wu_ref, wg_ref, cw_ref, wo_ref, o_ref,
               h_scr, acc_scr, tail_scr, work_scr, *, seq):
    i = pl.program_id(0)
    j = pl.program_id(1)
    tm = x_ref.shape[0]

    @pl.when(j == 0)
    def _():
        _normed_rows(x_ref, g_ref, h_scr)
        acc_scr[...] = x_ref[...]

    h = h_scr[...]
    cu = (jnp.dot(h, wc_ref[...], preferred_element_type=F32) * jnp.dot(h, wu_ref[...], preferred_element_type=F32))

    @pl.when(((i * tm) % seq == 0) & (j == 0))
    def _():
        tail_scr[...] = jnp.zeros(tail_scr.shape, F32)

    y = _causal_conv(cu, cw_ref[...], tail_scr.at[j], work_scr)
    y = y * jnp.dot(h, wb_ref[...], preferred_element_type=F32)
    y = y * _silu(jnp.dot(h, wg_ref[...], preferred_element_type=F32))
    acc_scr[...] += jnp.dot(y.astype(BF16), wo_ref[...], preferred_element_type=F32)

    @pl.when(j == pl.num_programs(1) - 1)
    def _():
        o_ref[...] = acc_scr[...]


def _shortconv_layer(x, g, w_in, conv_w, w_out):
    b, t, d = x.shape
    m = b * t
    x2d = x.reshape(m, d)
    w = w_in.shape[1] // 4
    tm = min(512, t)
    tn = min(512, w)
    nj = w // tn
    w_bf = w_in.astype(BF16)
    taps = conv_w.shape[0]

    def wspec(part):
        return pl.BlockSpec((d, tn), lambda i, j: (0, part * nj + j))

    out = pl.pallas_call(
        functools.partial(_sc_kernel, seq=t),
        grid=(m // tm, nj),
        in_specs=[pl.BlockSpec((tm, d), lambda i, j: (i, 0)),
                  pl.BlockSpec((1, d), lambda i, j: (0, 0)),
                  wspec(0), wspec(1), wspec(2), wspec(3),
                  pl.BlockSpec((taps, tn), lambda i, j: (0, j)),
                  pl.BlockSpec((tn, d), lambda i, j: (j, 0))],
        out_specs=pl.BlockSpec((tm, d), lambda i, j: (i, 0)),
        out_shape=jax.ShapeDtypeStruct((m, d), F32),
        scratch_shapes=[pltpu.VMEM((tm, d), BF16), pltpu.VMEM((tm, d), F32),
                        pltpu.VMEM((nj, SUBLANES, tn), F32), pltpu.VMEM((SUBLANES + tm, tn), F32)],
        compiler_params=_cparams("arbitrary", "arbitrary"),
        name="shortconv_layer",
    )(x2d, g.reshape(1, d), w_bf, w_bf, w_bf, w_bf, conv_w, w_out.astype(BF16))
    return out.reshape(b, t, d)


def kernel(x, norm_g, dn_w_in, dn_conv_w, dn_a_log, dn_dt_bias, dn_o_norm_g, dn_w_out, sb_w_in, sb_q_norm_g,
           sb_k_norm_g, sb_w_out, sc_w_in, sc_conv_w, sc_w_out):
    depth = norm_g.shape[0]
    n_mixers = 3
    for i in range(depth):
        j = i // n_mixers
        kind = i % n_mixers
        if kind == 0:
            x = _deltanet_layer(x, norm_g[i], dn_w_in[j], dn_conv_w[j], dn_a_log[j], dn_dt_bias[j],
                                dn_o_norm_g[j], dn_w_out[j])
        elif kind == 1:
            x = _stickbreak_layer(x, norm_g[i], sb_w_in[j], sb_q_norm_g[j], sb_k_norm_g[j], sb_w_out[j])
        else:
            x = _shortconv_layer(x, norm_g[i], sc_w_in[j], sc_conv_w[j], sc_w_out[j])
    return x
```

```python
import functools

import jax
import jax.numpy as jnp
from jax import lax
from jax.experimental import pallas as pl
from jax.experimental.pallas import tpu as pltpu

F32 = jnp.float32
BF16 = jnp.bfloat16
HIGHEST = lax.Precision.HIGHEST

SIGN_BIT = 0x80000000
RMS_EPS = 1e-6
L2_EPS = 1e-6
DN_HEADS = 8
DN_DK = 128
DN_DV = 256
DN_CHUNK = 64
DN_PREP_ROWS = 2048
DN_PREP_PASSES = 1
SB_DH = 64
LANES = 128
SUBLANES = 8
VMEM_LIMIT = 56 * 1024 * 1024

_NT = (((1,), (1,)), ((), ()))
_TN = (((0,), (0,)), ((), ()))


def _cparams(*sem):
    return pltpu.CompilerParams(dimension_semantics=sem, vmem_limit_bytes=VMEM_LIMIT)


def _silu(y):
    half = 0.5 * y
    return half + half * jnp.tanh(half)


def _softplus(y):
    return jnp.maximum(y, 0.0) + jnp.log1p(jnp.exp(-jnp.abs(y)))


def _normed_rows(x_ref, g_ref, h_scr):
    x = x_ref[...]
    ms = jnp.mean(x * x, axis=-1, keepdims=True)
    h_scr[...] = (x * lax.rsqrt(ms + RMS_EPS) * g_ref[...]).astype(h_scr.dtype)


def _causal_conv(acc, cw, tail_ref, work_ref):
    tm = acc.shape[0]
    taps = cw.shape[0]
    work_ref[0:SUBLANES, :] = tail_ref[...]
    work_ref[SUBLANES:SUBLANES + tm, :] = acc
    tail_ref[...] = acc[tm - SUBLANES:tm, :]
    y = acc * cw[taps - 1:taps, :]
    for s in range(1, taps):
        y = y + work_ref[SUBLANES - s:SUBLANES - s + tm, :] * cw[taps - 1 - s:taps - s, :]
    return y


def _group_sum_sq(y, group):
    width = 2 * LANES
    r = lax.broadcasted_iota(jnp.int32, (width, width), 0) // group
    c = lax.broadcasted_iota(jnp.int32, (width, width), 1) // group
    ones = jnp.where(r == c, 1.0, 0.0).astype(BF16)
    sq = (y * y).astype(BF16)
    parts = [jnp.dot(sq[:, s:s + width], ones, preferred_element_type=F32) for s in range(0, y.shape[1], width)]
    return jnp.concatenate(parts, axis=-1)


IN_CHUNK = 512


def _in_proj_kernel(*refs, plan, seq, n_out, n_conv, has_gain):
    refs = list(refs)
    x_ref, g_ref, w_ref = refs[:3]
    pos = 3
    cw_ref = gn_ref = tail_scr = work_scr = None
    if n_conv:
        cw_ref = refs[pos]
        pos += 1
    if has_gain:
        gn_ref = refs[pos]
        pos += 1
    out_refs = refs[pos:pos + n_out]
    if n_conv:
        tail_scr, work_scr = refs[pos + n_out:pos + n_out + 2]
    tm = x_ref.shape[0]
    x = x_ref[...]
    ms = jnp.mean(x * x, axis=-1, keepdims=True)
    h = (x * lax.rsqrt(ms + RMS_EPS) * g_ref[...]).astype(BF16)

    if n_conv:
        @pl.when((pl.program_id(0) * tm) % seq == 0)
        def _():
            tail_scr[...] = jnp.zeros(tail_scr.shape, F32)

    for c, (mode, out, off, group, scale) in enumerate(plan):
        cols = slice(c * IN_CHUNK, (c + 1) * IN_CHUNK)
        acc = jnp.dot(h, w_ref[:, cols], preferred_element_type=F32)
        if mode == "plain":
            y = acc
        elif mode == "silu":
            y = _silu(acc)
        elif mode in ("conv", "conv_l2"):
            y = _silu(_causal_conv(acc, cw_ref[:, cols], tail_scr.at[c], work_scr.at[c]))
            if mode == "conv_l2":
                y = y * (lax.rsqrt(_group_sum_sq(y, group) + L2_EPS) * scale)
        elif mode == "rms":
            y = acc * lax.rsqrt(_group_sum_sq(acc, group) * (1.0 / group) + RMS_EPS) * (gn_ref[:, cols] * scale)
        out_refs[out][:, off:off + IN_CHUNK] = y.astype(out_refs[out].dtype)


def _in_proj(x2d, g, w, plan, out_widths, *, tm, seq, conv_w=None, gain=None):
    m, d = x2d.shape
    n = w.shape[1]
    assert m % tm == 0 and seq % tm == 0 and n == len(plan) * IN_CHUNK
    n_conv = sum(1 for p in plan if p[0].startswith("conv"))
    assert all(p[0].startswith("conv") for p in plan[:n_conv])
    const = lambda i: (0, 0)
    in_specs = [pl.BlockSpec((tm, d), lambda i: (i, 0)), pl.BlockSpec((1, d), const), pl.BlockSpec((d, n), const)]
    args = [x2d, g.reshape(1, d), w]
    scratch = []
    if n_conv:
        in_specs.append(pl.BlockSpec(conv_w.shape, const))
        args.append(conv_w)
        scratch = [pltpu.VMEM((n_conv, SUBLANES, IN_CHUNK), F32), pltpu.VMEM((n_conv, SUBLANES + tm, IN_CHUNK), F32)]
    if gain is not None:
        in_specs.append(pl.BlockSpec(gain.shape, const))
        args.append(gain)
    return pl.pallas_call(
        functools.partial(_in_proj_kernel, plan=tuple(plan), seq=seq, n_out=len(out_widths), n_conv=n_conv,
                          has_gain=gain is not None),
        grid=(m // tm,),
        in_specs=in_specs,
        out_specs=[pl.BlockSpec((tm, ow), lambda i: (i, 0)) for ow in out_widths],
        out_shape=[jax.ShapeDtypeStruct((m, ow), BF16) for ow in out_widths],
        scratch_shapes=scratch,
        compiler_params=_cparams("arbitrary"),
        name="in_proj",
    )(*args)


def _out_kernel(a_ref, w_ref, x_ref, o_ref):
    o_ref[...] = x_ref[...] + jnp.dot(a_ref[...], w_ref[...], preferred_element_type=F32)


def _out_proj(a, w, x2d, *, tm):
    m, k = a.shape
    d = w.shape[1]
    return pl.pallas_call(
        _out_kernel,
        grid=(m // tm,),
        in_specs=[pl.BlockSpec((tm, k), lambda i: (i, 0)),
                  pl.BlockSpec((k, d), lambda i: (0, 0)),
                  pl.BlockSpec((tm, d), lambda i: (i, 0))],
        out_specs=pl.BlockSpec((tm, d), lambda i: (i, 0)),
        out_shape=jax.ShapeDtypeStruct((m, d), F32),
        compiler_params=_cparams("arbitrary"),
        name="out_proj",
    )(a, w, x2d)


def _ab_kernel(x_ref, g_ref, w3_ref, p_ref, oc_ref, or_ref, *, heads, chunk):
    x = x_ref[...]
    tm = x.shape[0]
    ms = jnp.mean(x * x, axis=-1, keepdims=True)
    hh, hl = _split(x * lax.rsqrt(ms + RMS_EPS) * g_ref[...])
    acc = jnp.dot(jnp.concatenate([hh, hl, hh], axis=1), w3_ref[...], preferred_element_type=F32)
    la = -jnp.exp(p_ref[0:1, :]) * _softplus(acc + p_ref[1:2, :])
    beta = jax.nn.sigmoid(acc)
    r = lax.broadcasted_iota(jnp.int32, (tm, tm), 0)
    c = lax.broadcasted_iota(jnp.int32, (tm, tm), 1)
    tril = jnp.where(((r // chunk) == (c // chunk)) & (c <= r), 1.0, 0.0).astype(BF16)
    lah, lal = _split(la)
    g_cum = jnp.dot(jnp.concatenate([tril, tril], axis=1), jnp.concatenate([lah, lal], axis=0),
                    preferred_element_type=F32)
    out = jnp.where(lax.broadcasted_iota(jnp.int32, acc.shape, 1) < heads, g_cum, beta)
    oc_ref[...] = out[:, :2 * heads]
    or_ref[...] = out.T[:2 * heads, :]


def _dn_ab(x2d, g, w_ab, a_log, dt_bias, *, tm):
    m, d = x2d.shape
    heads = a_log.shape[0]
    pad = LANES - 2 * heads
    w_pad = jnp.pad(w_ab, ((0, 0), (0, pad)))
    w_hi = w_pad.astype(BF16)
    w_lo = (w_pad - w_hi.astype(F32)).astype(BF16)
    w3 = jnp.concatenate([w_hi, w_hi, w_lo], axis=0)
    params = jnp.stack([jnp.pad(a_log, (0, LANES - heads)), jnp.pad(dt_bias, (0, LANES - heads))])
    return pl.pallas_call(
        functools.partial(_ab_kernel, heads=heads, chunk=DN_CHUNK),
        grid=(m // tm,),
        in_specs=[pl.BlockSpec((tm, d), lambda i: (i, 0)),
                  pl.BlockSpec((1, d), lambda i: (0, 0)),
                  pl.BlockSpec((3 * d, LANES), lambda i: (0, 0)),
                  pl.BlockSpec((2, LANES), lambda i: (0, 0))],
        out_specs=[pl.BlockSpec((tm, 2 * heads), lambda i: (i, 0)),
                   pl.BlockSpec((2 * heads, tm), lambda i: (0, i))],
        out_shape=[jax.ShapeDtypeStruct((m, 2 * heads), F32), jax.ShapeDtypeStruct((2 * heads, m), F32)],
        compiler_params=_cparams("arbitrary"),
        name="dn_ab",
    )(x2d, g.reshape(1, d), w3, params)


def _split(a):
    hi = a.astype(BF16)
    return hi, (a - hi.astype(F32)).astype(BF16)


def _mm(a, b, dims, passes):
    dg = functools.partial(lax.dot_general, dimension_numbers=dims, preferred_element_type=F32)
    if passes == 6:
        return dg(a, b, precision=HIGHEST)
    if passes == 1:
        return dg(a.astype(BF16), b.astype(BF16))
    ah, al = _split(a)
    bh, bl = _split(b)
    return dg(ah, bh) + (dg(ah, bl) + dg(al, bh))


_NN = (((1,), (0,)), ((), ()))


def _dn_prep_kernel(q_ref, k_ref, v_ref, gbc_ref, gbr_ref,
                    u_ref, w_ref, qd_ref, kd_ref, intra_ref, dec_ref, *, heads, chunk, passes):
    h = pl.program_id(1)
    tc = q_ref.shape[1]
    units = range(tc // chunk)
    gbc = gbc_ref[0]
    lane = lax.broadcasted_iota(jnp.int32, gbc.shape, 1)
    g_col = jnp.sum(jnp.where(lane == h, gbc, 0.0), axis=-1, keepdims=True)
    beta_col = jnp.sum(jnp.where(lane == h + heads, gbc, 0.0), axis=-1, keepdims=True)
    g_row = gbr_ref[pl.ds(h, 1), :]

    ri = lax.broadcasted_iota(jnp.int32, (chunk, chunk), 0)
    ci = lax.broadcasted_iota(jnp.int32, (chunk, chunk), 1)
    causal = ci <= ri
    strict = ci < ri
    eye = jnp.where(ci == ri, 1.0, 0.0).astype(F32)
    rows = [slice(n * chunk, (n + 1) * chunk) for n in units]

    ks = [k_ref[0, rs, :] for rs in rows]
    qs = [q_ref[0, rs, :] for rs in rows]
    gcs = [g_col[rs] for rs in rows]
    bcs = [beta_col[rs] for rs in rows]
    kbs = [ks[n] * bcs[n] for n in units]
    egs = [jnp.exp(gcs[n]) for n in units]
    decays = [jnp.where(causal, jnp.exp(jnp.where(causal, gcs[n] - g_row[:, rows[n]], 0.0)), 0.0) for n in units]
    kks = [_mm(kbs[n], ks[n], _NT, passes) for n in units]
    qks = [_mm(qs[n], ks[n], _NT, 1) for n in units]
    for n in units:
        intra_ref[0, 0, rows[n], :] = jnp.where(causal, qks[n] * decays[n], 0.0).astype(intra_ref.dtype)
        qd_ref[0, rows[n], :] = (qs[n] * egs[n]).astype(qd_ref.dtype)
        g_last = gcs[n][chunk - 1:chunk, :]
        kd_ref[0, rows[n], :] = (ks[n] * jnp.exp(g_last - gcs[n])).astype(kd_ref.dtype)
        grp = dec_ref.shape[3]
        dec_ref[0, 0, n // grp, n % grp:n % grp + 1, :] = jnp.broadcast_to(jnp.exp(g_last), (1, dec_ref.shape[-1]))

    ls = [jnp.where(strict, kks[n] * decays[n], 0.0) for n in units]

    def quarter(b):
        return ((ri // (2 * b)) == (ci // (2 * b))) & ((ri % (2 * b)) >= b) & ((ci % (2 * b)) < b)

    tinvs = [eye - jnp.where(quarter(1), ls[n], 0.0) for n in units]
    b = 2
    while b < chunk:
        cs = [jnp.where(quarter(b), ls[n], 0.0) for n in units]
        tcs = [_mm(tinvs[n], cs[n], _NN, passes) for n in units]
        tinvs = [tinvs[n] - _mm(tcs[n], tinvs[n], _NN, passes) for n in units]
        b *= 2
    for n in units:
        u_ref[0, rows[n], :] = _mm(tinvs[n], v_ref[0, rows[n], :] * bcs[n], _NN, passes).astype(u_ref.dtype)
    for n in units:
        w_ref[0, rows[n], :] = _mm(tinvs[n], kbs[n] * egs[n], _NN, passes).astype(w_ref.dtype)


def _dn_prep(q, k, v, ab_c, ab_r, *, tc, dec_group, passes):
    b, t, qw = q.shape
    heads = qw // DN_DK
    nt = t // tc
    grp = dec_group
    ngrp = tc // (DN_CHUNK * grp)
    qk_spec = pl.BlockSpec((1, tc, DN_DK), lambda bi, hi, ti: (bi, ti, hi))
    v_spec = pl.BlockSpec((1, tc, DN_DV), lambda bi, hi, ti: (bi, ti, hi))
    return pl.pallas_call(
        functools.partial(_dn_prep_kernel, heads=heads, chunk=DN_CHUNK, passes=passes),
        grid=(b, heads, nt),
        in_specs=[qk_spec, qk_spec, v_spec,
                  pl.BlockSpec((1, tc, 2 * heads), lambda bi, hi, ti: (bi, ti, 0)),
                  pl.BlockSpec((2 * heads, tc), lambda bi, hi, ti: (0, bi * nt + ti))],
        out_specs=[v_spec, qk_spec, qk_spec, qk_spec,
                   pl.BlockSpec((1, 1, tc, DN_CHUNK), lambda bi, hi, ti: (bi, hi, ti, 0)),
                   pl.BlockSpec((1, 1, ngrp, grp, DN_DV), lambda bi, hi, ti: (bi, hi, ti, 0, 0))],
        out_shape=[jax.ShapeDtypeStruct((b, t, heads * DN_DV), BF16),
                   jax.ShapeDtypeStruct((b, t, qw), BF16),
                   jax.ShapeDtypeStruct((b, t, qw), BF16),
                   jax.ShapeDtypeStruct((b, t, qw), BF16),
                   jax.ShapeDtypeStruct((b, heads, t, DN_CHUNK), BF16),
                   jax.ShapeDtypeStruct((b, heads, nt * ngrp, grp, DN_DV), F32)],
        compiler_params=_cparams("arbitrary", "arbitrary", "arbitrary"),
        name="dn_prep",
    )(q, k, v, ab_c, ab_r)


def _dn_scan_kernel(u_ref, w_ref, qd_ref, kd_ref, intra_ref, dec_ref, o_ref, s_scr, *, heads, chunk):
    nchunk = u_ref.shape[1] // chunk

    @pl.when(pl.program_id(1) == 0)
    def _():
        s_scr[...] = jnp.zeros(s_scr.shape, F32)

    def body(n, carry):
        rows = pl.ds(pl.multiple_of(n * chunk, chunk), chunk)
        kcols = [slice(h * DN_DK, (h + 1) * DN_DK) for h in range(heads)]
        vcols = [slice(h * DN_DV, (h + 1) * DN_DV) for h in range(heads)]
        ss = [s_scr[h] for h in range(heads)]
        sbs = [s.astype(BF16) for s in ss]
        wq = [jnp.dot(jnp.concatenate([w_ref[0, rows, kcols[h]], qd_ref[0, rows, kcols[h]]], axis=0), sbs[h],
                      preferred_element_type=F32) for h in range(heads)]
        vbs = [(u_ref[0, rows, vcols[h]] - wq[h][:chunk]).astype(BF16) for h in range(heads)]
        for h in range(heads):
            o_ref[0, rows, vcols[h]] = (wq[h][chunk:] + jnp.dot(intra_ref[0, h, rows, :], vbs[h],
                                                                preferred_element_type=F32)).astype(o_ref.dtype)
        for h in range(heads):
            s_scr[h] = (ss[h] * dec_ref[0, h, 0, pl.ds(n, 1), :]
                        + lax.dot_general(kd_ref[0, rows, kcols[h]], vbs[h], _TN, preferred_element_type=F32))
        return carry

    lax.fori_loop(0, nchunk, body, 0)


def _dn_scan(u, w, qd, kd, intra, dec, *, tc):
    b, t, vw = u.shape
    heads = vw // DN_DV
    qw = heads * DN_DK
    nt = t // tc
    nchunk = tc // DN_CHUNK
    assert dec.shape == (b, heads, nt, nchunk, DN_DV)
    return pl.pallas_call(
        functools.partial(_dn_scan_kernel, heads=heads, chunk=DN_CHUNK),
        grid=(b, nt),
        in_specs=[pl.BlockSpec((1, tc, vw), lambda bi, ti: (bi, ti, 0)),
                  pl.BlockSpec((1, tc, qw), lambda bi, ti: (bi, ti, 0)),
                  pl.BlockSpec((1, tc, qw), lambda bi, ti: (bi, ti, 0)),
                  pl.BlockSpec((1, tc, qw), lambda bi, ti: (bi, ti, 0)),
                  pl.BlockSpec((1, heads, tc, DN_CHUNK), lambda bi, ti: (bi, 0, ti, 0)),
                  pl.BlockSpec((1, heads, 1, nchunk, DN_DV), lambda bi, ti: (bi, 0, ti, 0, 0))],
        out_specs=pl.BlockSpec((1, tc, vw), lambda bi, ti: (bi, ti, 0)),
        out_shape=jax.ShapeDtypeStruct((b, t, vw), BF16),
        scratch_shapes=[pltpu.VMEM((heads, DN_DK, DN_DV), F32)],
        compiler_params=_cparams("arbitrary", "arbitrary"),
        name="dn_scan",
    )(u, w, qd, kd, intra, dec)


def _dn_out_kernel(o_ref, gate_ref, gn_ref, w_ref, x_ref, out_ref, *, heads):
    parts = []
    for h in range(heads):
        cols = slice(h * DN_DV, (h + 1) * DN_DV)
        o = o_ref[:, cols].astype(F32)
        ms = jnp.mean(o * o, axis=-1, keepdims=True)
        parts.append((o * lax.rsqrt(ms + RMS_EPS) * gn_ref[...] * gate_ref[:, cols].astype(F32)).astype(BF16))
    out_ref[...] = x_ref[...] + jnp.dot(jnp.concatenate(parts, axis=1), w_ref[...], preferred_element_type=F32)


def _dn_out(o2d, gate, gn, w_out, x2d, *, tm):
    m, vw = o2d.shape
    d = x2d.shape[1]
    heads = vw // DN_DV
    return pl.pallas_call(
        functools.partial(_dn_out_kernel, heads=heads),
        grid=(m // tm,),
        in_specs=[pl.BlockSpec((tm, vw), lambda i: (i, 0)),
                  pl.BlockSpec((tm, vw), lambda i: (i, 0)),
                  pl.BlockSpec((1, DN_DV), lambda i: (0, 0)),
                  pl.BlockSpec((vw, d), lambda i: (0, 0)),
                  pl.BlockSpec((tm, d), lambda i: (i, 0))],
        out_specs=pl.BlockSpec((tm, d), lambda i: (i, 0)),
        out_shape=jax.ShapeDtypeStruct((m, d), F32),
        compiler_params=_cparams("arbitrary"),
        name="dn_out",
    )(o2d, gate, gn.reshape(1, DN_DV), w_out, x2d)


def _deltanet_layer(x, g, w_in, conv_w, a_log, dt_bias, o_norm_g, w_out):
    b, t, d = x.shape
    m = b * t
    x2d = x.reshape(m, d)
    heads = a_log.shape[0]
    qkw = heads * DN_DK
    vw = heads * DN_DV
    tm = min(512, t)
    nq, nv = qkw // IN_CHUNK, vw // IN_CHUNK
    plan = ([("conv_l2", 0, c * IN_CHUNK, DN_DK, DN_DK ** -0.5) for c in range(nq)]
            + [("conv_l2", 1, c * IN_CHUNK, DN_DK, 1.0) for c in range(nq)]
            + [("conv", 2, c * IN_CHUNK, 0, 1.0) for c in range(nv)]
            + [("silu", 3, c * IN_CHUNK, 0, 1.0) for c in range(nv)])
    q, k, v, gate = _in_proj(x2d, g, w_in[:, :2 * qkw + 2 * vw].astype(BF16), plan, (qkw, qkw, vw, vw), tm=tm,
                             seq=t, conv_w=conv_w)
    ab_c, ab_r = _dn_ab(x2d, g, w_in[:, 2 * qkw + 2 * vw:], a_log, dt_bias, tm=tm)
    tc = min(512, t)
    u, w, qd, kd, intra, dec = _dn_prep(q.reshape(b, t, qkw), k.reshape(b, t, qkw), v.reshape(b, t, vw),
                                        ab_c.reshape(b, t, 2 * heads), ab_r, tc=min(DN_PREP_ROWS, t),
                                        dec_group=tc // DN_CHUNK, passes=DN_PREP_PASSES)
    o = _dn_scan(u, w, qd, kd, intra, dec, tc=tc)
    out = _dn_out(o.reshape(m, vw), gate, o_norm_g, w_out.astype(BF16), x2d, tm=tm)
    return out.reshape(b, t, d)


def _sb_kernel(q_ref, k_ref, v_ref, gate_ref, o_ref, zn_s, l1mb_s, wts_s, sum_s, run_s, acc_s, *, dh):
    i = pl.program_id(2)
    tq = q_ref.shape[1]
    tk = tq
    nh = LANES // dh
    nlb = q_ref.shape[2] // LANES
    lane = lax.broadcasted_iota(jnp.int32, (tq, LANES), 1)
    ri = lax.broadcasted_iota(jnp.int32, (tk, tk), 0)
    ci = lax.broadcasted_iota(jnp.int32, (tk, tk), 1)
    later = jnp.where(ri >= ci, 1.0, 0.0).astype(BF16)
    mask = ci < ri

    chains = [(lb, hh) for lb in range(nlb) for hh in range(nh)]
    nc = len(chains)
    qms = []
    for lb, hh in chains:
        q = q_ref[0, :, lb * LANES:(lb + 1) * LANES]
        qms.append(jnp.where((lane >= hh * dh) & (lane < (hh + 1) * dh), q, jnp.zeros_like(q)))

    def kv_block(ref, s):
        start = pl.multiple_of(jnp.maximum(i - s, 0) * tk, tk)
        return [ref[0, pl.ds(start, tk), lb * LANES:(lb + 1) * LANES] for lb in range(nlb)]

    def stage_a(s, slot):
        kjs = kv_block(k_ref, s)
        for c in range(nc):
            zn_s[slot, c] = lax.dot_general(qms[c], kjs[chains[c][0]], _NT, preferred_element_type=F32)

    def stage_b(slot, diag):
        for c in range(nc):
            zn = zn_s[slot, c]
            neg_abs = lax.bitcast_convert_type(lax.bitcast_convert_type(zn, jnp.uint32) | jnp.uint32(SIGN_BIT), F32)
            l1m = jnp.minimum(zn, 0.0) - jnp.log(1.0 + jnp.exp(neg_abs))
            if diag:
                l1m = jnp.where(mask, l1m, 0.0)
            l1mb_s[slot, c] = l1m.astype(BF16)
            sum_s[slot, c] = jnp.broadcast_to(jnp.sum(l1m, axis=-1, keepdims=True), (tq, LANES))

    def stage_c(slot):
        return [jnp.dot(l1mb_s[slot, c], later, preferred_element_type=F32) for c in range(nc)]

    def stage_d(cums, slot, diag):
        for c in range(nc):
            run = run_s[c]
            w = jnp.exp(cums[c] - zn_s[slot, c] + jnp.concatenate([run] * (tk // LANES), axis=1))
            if diag:
                w = jnp.where(mask, w, 0.0)
            wts_s[slot, c] = w.astype(BF16)
            run_s[c] = run + sum_s[slot, c]

    def stage_e(s, slot):
        vjs = kv_block(v_ref, s)
        for c in range(nc):
            acc_s[c] += jnp.dot(wts_s[slot, c], vjs[chains[c][0]], preferred_element_type=F32)

    run_s[...] = jnp.zeros(run_s.shape, F32)
    acc_s[...] = jnp.zeros(acc_s.shape, F32)
    stage_a(0, 0)
    stage_a(1, 1)
    stage_b(0, True)
    cum0 = stage_c(0)
    stage_b(1, False)
    stage_d(cum0, 0, True)
    stage_a(2, 0)

    def trip(t, par):
        cums = stage_c(par)
        stage_e(t - 1, 1 - par)
        stage_b(1 - par, False)
        stage_d(cums, par, False)
        stage_a(t + 2, par)

    def pair(p, carry):
        trip(2 * p + 1, 1)
        trip(2 * p + 2, 0)
        return carry

    lax.fori_loop(0, i // 2, pair, 0)

    @pl.when(i % 2 == 1)
    def _():
        trip(i, 1)

    @pl.when(i % 2 == 1)
    def _():
        stage_e(i, 1)

    @pl.when(i % 2 == 0)
    def _():
        stage_e(i, 0)

    for c in range(0, nc, nh):
        lb = chains[c][0]
        o = acc_s[c]
        for hh in range(1, nh):
            o = jnp.where(lane >= hh * dh, acc_s[c + hh], o)
        cs = slice(lb * LANES, (lb + 1) * LANES)
        o_ref[0, :, cs] = (o * gate_ref[0, :, cs].astype(F32)).astype(o_ref.dtype)


def _sb_attention(q, k, v, gate, *, tq, bw):
    b, t, w = q.shape
    nc = bw // SB_DH
    blk = pl.BlockSpec((1, tq, bw), lambda bi, hi, ti: (bi, ti, hi))
    full = pl.BlockSpec((1, t, bw), lambda bi, hi, ti: (bi, 0, hi))
    return pl.pallas_call(
        functools.partial(_sb_kernel, dh=SB_DH),
        grid=(b, w // bw, t // tq),
        in_specs=[blk, full, full, blk],
        out_specs=blk,
        out_shape=jax.ShapeDtypeStruct((b, t, w), BF16),
        scratch_shapes=[pltpu.VMEM((2, nc, tq, tq), F32),
                        pltpu.VMEM((2, nc, tq, tq), BF16),
                        pltpu.VMEM((2, nc, tq, tq), BF16),
                        pltpu.VMEM((2, nc, tq, LANES), F32),
                        pltpu.VMEM((nc, tq, LANES), F32),
                        pltpu.VMEM((nc, tq, LANES), F32)],
        compiler_params=_cparams("arbitrary", "arbitrary", "arbitrary"),
        name="sb_attn",
    )(q, k, v, gate)


def _stickbreak_layer(x, g, w_in, q_norm_g, k_norm_g, w_out):
    b, t, d = x.shape
    m = b * t
    x2d = x.reshape(m, d)
    w = w_in.shape[1] // 4
    heads = w // SB_DH
    tm = min(512, t)
    gains = jnp.concatenate([jnp.tile(q_norm_g, heads), jnp.tile(k_norm_g, heads)]).reshape(1, 2 * w)
    nw = w // IN_CHUNK
    plan = ([("rms", 0, c * IN_CHUNK, SB_DH, -(SB_DH ** -0.5)) for c in range(nw)]
            + [("rms", 1, c * IN_CHUNK, SB_DH, 1.0) for c in range(nw)]
            + [("plain", 2, c * IN_CHUNK, 0, 1.0) for c in range(nw)]
            + [("silu", 3, c * IN_CHUNK, 0, 1.0) for c in range(nw)])
    q, k, v, gate = _in_proj(x2d, g, w_in.astype(BF16), plan, (w, w, w, w), tm=tm, seq=t, gain=gains)
    o = _sb_attention(q.reshape(b, t, w), k.reshape(b, t, w), v.reshape(b, t, w), gate.reshape(b, t, w),
                      tq=min(256, t), bw=512)
    out = _out_proj(o.reshape(m, w), w_out.astype(BF16), x2d, tm=tm)
    return out.reshape(b, t, d)


def _sc_kernel(x_ref, g_ref, wb_ref, wc_ref, wu_ref, wg_ref, cw_ref, wo_ref, o_ref,
               h_scr, acc_scr, tail_scr, work_scr, *, seq):
    i = pl.program_id(0)
    j = pl.program_id(1)
    tm = x_ref.shape[0]

    @pl.when(j == 0)
    def _():
        _normed_rows(x_ref, g_ref, h_scr)
        acc_scr[...] = x_ref[...]

    h = h_scr[...]
    cu = (jnp.dot(h, wc_ref[...], preferred_element_type=F32) * jnp.dot(h, wu_ref[...], preferred_element_type=F32))

    @pl.when(((i * tm) % seq == 0) & (j == 0))
    def _():
        tail_scr[...] = jnp.zeros(tail_scr.shape, F32)

    y = _causal_conv(cu, cw_ref[...], tail_scr.at[j], work_scr)
    y = y * jnp.dot(h, wb_ref[...], preferred_element_type=F32)
    y = y * _silu(jnp.dot(h, wg_ref[...], preferred_element_type=F32))
    acc_scr[...] += jnp.dot(y.astype(BF16), wo_ref[...], preferred_element_type=F32)

    @pl.when(j == pl.num_programs(1) - 1)
    def _():
        o_ref[...] = acc_scr[...]


def _shortconv_layer(x, g, w_in, conv_w, w_out):
    b, t, d = x.shape
    m = b * t
    x2d = x.reshape(m, d)
    w = w_in.shape[1] // 4
    tm = min(512, t)
    tn = min(512, w)
    nj = w // tn
    w_bf = w_in.astype(BF16)
    taps = conv_w.shape[0]

    def wspec(part):
        return pl.BlockSpec((d, tn), lambda i, j: (0, part * nj + j))

    out = pl.pallas_call(
        functools.partial(_sc_kernel, seq=t),
        grid=(m // tm, nj),
        in_specs=[pl.BlockSpec((tm, d), lambda i, j: (i, 0)),
                  pl.BlockSpec((1, d), lambda i, j: (0, 0)),
                  wspec(0), wspec(1), wspec(2), wspec(3),
                  pl.BlockSpec((taps, tn), lambda i, j: (0, j)),
                  pl.BlockSpec((tn, d), lambda i, j: (j, 0))],
        out_specs=pl.BlockSpec((tm, d), lambda i, j: (i, 0)),
        out_shape=jax.ShapeDtypeStruct((m, d), F32),
        scratch_shapes=[pltpu.VMEM((tm, d), BF16), pltpu.VMEM((tm, d), F32),
                        pltpu.VMEM((nj, SUBLANES, tn), F32), pltpu.VMEM((SUBLANES + tm, tn), F32)],
        compiler_params=_cparams("arbitrary", "arbitrary"),
        name="shortconv_layer",
    )(x2d, g.reshape(1, d), w_bf, w_bf, w_bf, w_bf, conv_w, w_out.astype(BF16))
    return out.reshape(b, t, d)


def kernel(x, norm_g, dn_w_in, dn_conv_w, dn_a_log, dn_dt_bias, dn_o_norm_g, dn_w_out, sb_w_in, sb_q_norm_g,
           sb_k_norm_g, sb_w_out, sc_w_in, sc_conv_w, sc_w_out):
    depth = norm_g.shape[0]
    n_mixers = 3
    for i in range(depth):
        j = i // n_mixers
        kind = i % n_mixers
        if kind == 0:
            x = _deltanet_layer(x, norm_g[i], dn_w_in[j], dn_conv_w[j], dn_a_log[j], dn_dt_bias[j],
                                dn_o_norm_g[j], dn_w_out[j])
        elif kind == 1:
            x = _stickbreak_layer(x, norm_g[i], sb_w_in[j], sb_q_norm_g[j], sb_k_norm_g[j], sb_w_out[j])
        else:
            x = _shortconv_layer(x, norm_g[i], sc_w_in[j], sc_conv_w[j], sc_w_out[j])
    return x
```

```python
import functools

import jax
import jax.numpy as jnp
from jax import lax
from jax.experimental import pallas as pl
from jax.experimental.pallas import tpu as pltpu

F32 = jnp.float32
BF16 = jnp.bfloat16
HIGHEST = lax.Precision.HIGHEST

SIGN_BIT = 0x80000000
RMS_EPS = 1e-6
L2_EPS = 1e-6
DN_HEADS = 8
DN_DK = 128
DN_DV = 256
DN_CHUNK = 64
DN_PREP_ROWS = 2048
DN_PREP_PASSES = 1
SB_DH = 64
LANES = 128
SUBLANES = 8
VMEM_LIMIT = 56 * 1024 * 1024

_NT = (((1,), (1,)), ((), ()))
_TN = (((0,), (0,)), ((), ()))


def _cparams(*sem):
    return pltpu.CompilerParams(dimension_semantics=sem, vmem_limit_bytes=VMEM_LIMIT)


def _silu(y):
    half = 0.5 * y
    return half + half * jnp.tanh(half)


def _softplus(y):
    return jnp.maximum(y, 0.0) + jnp.log1p(jnp.exp(-jnp.abs(y)))


def _normed_rows(x_ref, g_ref, h_scr):
    x = x_ref[...]
    ms = jnp.mean(x * x, axis=-1, keepdims=True)
    h_scr[...] = (x * lax.rsqrt(ms + RMS_EPS) * g_ref[...]).astype(h_scr.dtype)


def _causal_conv(acc, cw, tail_ref, work_ref):
    tm = acc.shape[0]
    taps = cw.shape[0]
    work_ref[0:SUBLANES, :] = tail_ref[...]
    work_ref[SUBLANES:SUBLANES + tm, :] = acc
    tail_ref[...] = acc[tm - SUBLANES:tm, :]
    y = acc * cw[taps - 1:taps, :]
    for s in range(1, taps):
        y = y + work_ref[SUBLANES - s:SUBLANES - s + tm, :] * cw[taps - 1 - s:taps - s, :]
    return y


def _group_sum_sq(y, group):
    width = 2 * LANES
    r = lax.broadcasted_iota(jnp.int32, (width, width), 0) // group
    c = lax.broadcasted_iota(jnp.int32, (width, width), 1) // group
    ones = jnp.where(r == c, 1.0, 0.0).astype(BF16)
    sq = (y * y).astype(BF16)
    parts = [jnp.dot(sq[:, s:s + width], ones, preferred_element_type=F32) for s in range(0, y.shape[1], width)]
    return jnp.concatenate(parts, axis=-1)


IN_CHUNK = 512


def _in_proj_kernel(*refs, plan, seq, n_out, n_conv, has_gain):
    refs = list(refs)
    x_ref, g_ref, w_ref = refs[:3]
    pos = 3
    cw_ref = gn_ref = tail_scr = work_scr = None
    if n_conv:
        cw_ref = refs[pos]
        pos += 1
    if has_gain:
        gn_ref = refs[pos]
        pos += 1
    out_refs = refs[pos:pos + n_out]
    if n_conv:
        tail_scr, work_scr = refs[pos + n_out:pos + n_out + 2]
    tm = x_ref.shape[0]
    x = x_ref[...]
    ms = jnp.mean(x * x, axis=-1, keepdims=True)
    h = (x * lax.rsqrt(ms + RMS_EPS) * g_ref[...]).astype(BF16)

    if n_conv:
        @pl.when((pl.program_id(0) * tm) % seq == 0)
        def _():
            tail_scr[...] = jnp.zeros(tail_scr.shape, F32)

    for c, (mode, out, off, group, scale) in enumerate(plan):
        cols = slice(c * IN_CHUNK, (c + 1) * IN_CHUNK)
        acc = jnp.dot(h, w_ref[:, cols], preferred_element_type=F32)
        if mode == "plain":
            y = acc
        elif mode == "silu":
            y = _silu(acc)
        elif mode in ("conv", "conv_l2"):
            y = _silu(_causal_conv(acc, cw_ref[:, cols], tail_scr.at[c], work_scr.at[c]))
            if mode == "conv_l2":
                y = y * (lax.rsqrt(_group_sum_sq(y, group) + L2_EPS) * scale)
        elif mode == "rms":
            y = acc * lax.rsqrt(_group_sum_sq(acc, group) * (1.0 / group) + RMS_EPS) * (gn_ref[:, cols] * scale)
        out_refs[out][:, off:off + IN_CHUNK] = y.astype(out_refs[out].dtype)


def _in_proj(x2d, g, w, plan, out_widths, *, tm, seq, conv_w=None, gain=None):
    m, d = x2d.shape
    n = w.shape[1]
    assert m % tm == 0 and seq % tm == 0 and n == len(plan) * IN_CHUNK
    n_conv = sum(1 for p in plan if p[0].startswith("conv"))
    assert all(p[0].startswith("conv") for p in plan[:n_conv])
    const = lambda i: (0, 0)
    in_specs = [pl.BlockSpec((tm, d), lambda i: (i, 0)), pl.BlockSpec((1, d), const), pl.BlockSpec((d, n), const)]
    args = [x2d, g.reshape(1, d), w]
    scratch = []
    if n_conv:
        in_specs.append(pl.BlockSpec(conv_w.shape, const))
        args.append(conv_w)
        scratch = [pltpu.VMEM((n_conv, SUBLANES, IN_CHUNK), F32), pltpu.VMEM((n_conv, SUBLANES + tm, IN_CHUNK), F32)]
    if gain is not None:
        in_specs.append(pl.BlockSpec(gain.shape, const))
        args.append(gain)
    return pl.pallas_call(
        functools.partial(_in_proj_kernel, plan=tuple(plan), seq=seq, n_out=len(out_widths), n_conv=n_conv,
                          has_gain=gain is not None),
        grid=(m // tm,),
        in_specs=in_specs,
        out_specs=[pl.BlockSpec((tm, ow), lambda i: (i, 0)) for ow in out_widths],
        out_shape=[jax.ShapeDtypeStruct((m, ow), BF16) for ow in out_widths],
        scratch_shapes=scratch,
        compiler_params=_cparams("arbitrary"),
        name="in_proj",
    )(*args)


def _out_kernel(a_ref, w_ref, x_ref, o_ref):
    o_ref[...] = x_ref[...] + jnp.dot(a_ref[...], w_ref[...], preferred_element_type=F32)


def _out_proj(a, w, x2d, *, tm):
    m, k = a.shape
    d = w.shape[1]
    return pl.pallas_call(
        _out_kernel,
        grid=(m // tm,),
        in_specs=[pl.BlockSpec((tm, k), lambda i: (i, 0)),
                  pl.BlockSpec((k, d), lambda i: (0, 0)),
                  pl.BlockSpec((tm, d), lambda i: (i, 0))],
        out_specs=pl.BlockSpec((tm, d), lambda i: (i, 0)),
        out_shape=jax.ShapeDtypeStruct((m, d), F32),
        compiler_params=_cparams("arbitrary"),
        name="out_proj",
    )(a, w, x2d)


def _ab_kernel(x_ref, g_ref, w3_ref, p_ref, oc_ref, or_ref, *, heads, chunk):
    x = x_ref[...]
    tm = x.shape[0]
    ms = jnp.mean(x * x, axis=-1, keepdims=True)
    hh, hl = _split(x * lax.rsqrt(ms + RMS_EPS) * g_ref[...])
    acc = jnp.dot(jnp.concatenate([hh, hl, hh], axis=1), w3_ref[...], preferred_element_type=F32)
    la = -jnp.exp(p_ref[0:1, :]) * _softplus(acc + p_ref[1:2, :])
    beta = jax.nn.sigmoid(acc)
    r = lax.broadcasted_iota(jnp.int32, (tm, tm), 0)
    c = lax.broadcasted_iota(jnp.int32, (tm, tm), 1)
    tril = jnp.where(((r // chunk) == (c // chunk)) & (c <= r), 1.0, 0.0).astype(BF16)
    lah, lal = _split(la)
    g_cum = jnp.dot(jnp.concatenate([tril, tril], axis=1), jnp.concatenate([lah, lal], axis=0),
                    preferred_element_type=F32)
    out = jnp.where(lax.broadcasted_iota(jnp.int32, acc.shape, 1) < heads, g_cum, beta)
    oc_ref[...] = out[:, :2 * heads]
    or_ref[...] = out.T[:2 * heads, :]


def _dn_ab(x2d, g, w_ab, a_log, dt_bias, *, tm):
    m, d = x2d.shape
    heads = a_log.shape[0]
    pad = LANES - 2 * heads
    w_pad = jnp.pad(w_ab, ((0, 0), (0, pad)))
    w_hi = w_pad.astype(BF16)
    w_lo = (w_pad - w_hi.astype(F32)).astype(BF16)
    w3 = jnp.concatenate([w_hi, w_hi, w_lo], axis=0)
    params = jnp.stack([jnp.pad(a_log, (0, LANES - heads)), jnp.pad(dt_bias, (0, LANES - heads))])
    return pl.pallas_call(
        functools.partial(_ab_kernel, heads=heads, chunk=DN_CHUNK),
        grid=(m // tm,),
        in_specs=[pl.BlockSpec((tm, d), lambda i: (i, 0)),
                  pl.BlockSpec((1, d), lambda i: (0, 0)),
                  pl.BlockSpec((3 * d, LANES), lambda i: (0, 0)),
                  pl.BlockSpec((2, LANES), lambda i: (0, 0))],
        out_specs=[pl.BlockSpec((tm, 2 * heads), lambda i: (i, 0)),
                   pl.BlockSpec((2 * heads, tm), lambda i: (0, i))],
        out_shape=[jax.ShapeDtypeStruct((m, 2 * heads), F32), jax.ShapeDtypeStruct((2 * heads, m), F32)],
        compiler_params=_cparams("arbitrary"),
        name="dn_ab",
    )(x2d, g.reshape(1, d), w3, params)


def _split(a):
    hi = a.astype(BF16)
    return hi, (a - hi.astype(F32)).astype(BF16)


def _mm(a, b, dims, passes):
    dg = functools.partial(lax.dot_general, dimension_numbers=dims, preferred_element_type=F32)
    if passes == 6:
        return dg(a, b, precision=HIGHEST)
    if passes == 1:
        return dg(a.astype(BF16), b.astype(BF16))
    ah, al = _split(a)
    bh, bl = _split(b)
    return dg(ah, bh) + (dg(ah, bl) + dg(al, bh))


_NN = (((1,), (0,)), ((), ()))


def _dn_prep_kernel(q_ref, k_ref, v_ref, gbc_ref, gbr_ref,
                    u_ref, w_ref, qd_ref, kd_ref, intra_ref, dec_ref, *, heads, chunk, passes):
    h = pl.program_id(1)
    tc = q_ref.shape[1]
    units = range(tc // chunk)
    gbc = gbc_ref[0]
    lane = lax.broadcasted_iota(jnp.int32, gbc.shape, 1)
    g_col = jnp.sum(jnp.where(lane == h, gbc, 0.0), axis=-1, keepdims=True)
    beta_col = jnp.sum(jnp.where(lane == h + heads, gbc, 0.0), axis=-1, keepdims=True)
    g_row = gbr_ref[pl.ds(h, 1), :]

    ri = lax.broadcasted_iota(jnp.int32, (chunk, chunk), 0)
    ci = lax.broadcasted_iota(jnp.int32, (chunk, chunk), 1)
    causal = ci <= ri
    strict = ci < ri
    eye = jnp.where(ci == ri, 1.0, 0.0).astype(F32)
    rows = [slice(n * chunk, (n + 1) * chunk) for n in units]

    ks = [k_ref[0, rs, :] for rs in rows]
    qs = [q_ref[0, rs, :] for rs in rows]
    gcs = [g_col[rs] for rs in rows]
    bcs = [beta_col[rs] for rs in rows]
    kbs = [ks[n] * bcs[n] for n in units]
    egs = [jnp.exp(gcs[n]) for n in units]
    decays = [jnp.where(causal, jnp.exp(jnp.where(causal, gcs[n] - g_row[:, rows[n]], 0.0)), 0.0) for n in units]
    kks = [_mm(kbs[n], ks[n], _NT, passes) for n in units]
    qks = [_mm(qs[n], ks[n], _NT, 1) for n in units]
    for n in units:
        intra_ref[0, 0, rows[n], :] = jnp.where(causal, qks[n] * decays[n], 0.0).astype(intra_ref.dtype)
        qd_ref[0, rows[n], :] = (qs[n] * egs[n]).astype(qd_ref.dtype)
        g_last = gcs[n][chunk - 1:chunk, :]
        kd_ref[0, rows[n], :] = (ks[n] * jnp.exp(g_last - gcs[n])).astype(kd_ref.dtype)
        grp = dec_ref.shape[3]
        dec_ref[0, 0, n // grp, n % grp:n % grp + 1, :] = jnp.broadcast_to(jnp.exp(g_last), (1, dec_ref.shape[-1]))

    ls = [jnp.where(strict, kks[n] * decays[n], 0.0) for n in units]

    def quarter(b):
        return ((ri // (2 * b)) == (ci // (2 * b))) & ((ri % (2 * b)) >= b) & ((ci % (2 * b)) < b)

    tinvs = [eye - jnp.where(quarter(1), ls[n], 0.0) for n in units]
    b = 2
    while b < chunk:
        cs = [jnp.where(quarter(b), ls[n], 0.0) for n in units]
        tcs = [_mm(tinvs[n], cs[n], _NN, passes) for n in units]
        tinvs = [tinvs[n] - _mm(tcs[n], tinvs[n], _NN, passes) for n in units]
        b *= 2
    for n in units:
        u_ref[0, rows[n], :] = _mm(tinvs[n], v_ref[0, rows[n], :] * bcs[n], _NN, passes).astype(u_ref.dtype)
    for n in units:
        w_ref[0, rows[n], :] = _mm(tinvs[n], kbs[n] * egs[n], _NN, passes).astype(w_ref.dtype)


def _dn_prep(q, k, v, ab_c, ab_r, *, tc, dec_group, passes):
    b, t, qw = q.shape
    heads = qw // DN_DK
    nt = t // tc
    grp = dec_group
    ngrp = tc // (DN_CHUNK * grp)
    qk_spec = pl.BlockSpec((1, tc, DN_DK), lambda bi, hi, ti: (bi, ti, hi))
    v_spec = pl.BlockSpec((1, tc, DN_DV), lambda bi, hi, ti: (bi, ti, hi))
    return pl.pallas_call(
        functools.partial(_dn_prep_kernel, heads=heads, chunk=DN_CHUNK, passes=passes),
        grid=(b, heads, nt),
        in_specs=[qk_spec, qk_spec, v_spec,
                  pl.BlockSpec((1, tc, 2 * heads), lambda bi, hi, ti: (bi, ti, 0)),
                  pl.BlockSpec((2 * heads, tc), lambda bi, hi, ti: (0, bi * nt + ti))],
        out_specs=[v_spec, qk_spec, qk_spec, qk_spec,
                   pl.BlockSpec((1, 1, tc, DN_CHUNK), lambda bi, hi, ti: (bi, hi, ti, 0)),
                   pl.BlockSpec((1, 1, ngrp, grp, DN_DV), lambda bi, hi, ti: (bi, hi, ti, 0, 0))],
        out_shape=[jax.ShapeDtypeStruct((b, t, heads * DN_DV), BF16),
                   jax.ShapeDtypeStruct((b, t, qw), BF16),
                   jax.ShapeDtypeStruct((b, t, qw), BF16),
                   jax.ShapeDtypeStruct((b, t, qw), BF16),
                   jax.ShapeDtypeStruct((b, heads, t, DN_CHUNK), BF16),
                   jax.ShapeDtypeStruct((b, heads, nt * ngrp, grp, DN_DV), F32)],
        compiler_params=_cparams("arbitrary", "arbitrary", "arbitrary"),
        name="dn_prep",
    )(q, k, v, ab_c, ab_r)


def _dn_scan_kernel(u_ref, w_ref, qd_ref, kd_ref, intra_ref, dec_ref, o_ref, s_scr, *, heads, chunk):
    nchunk = u_ref.shape[1] // chunk

    @pl.when(pl.program_id(1) == 0)
    def _():
        s_scr[...] = jnp.zeros(s_scr.shape, F32)

    def body(n, carry):
        rows = pl.ds(pl.multiple_of(n * chunk, chunk), chunk)
        kcols = [slice(h * DN_DK, (h + 1) * DN_DK) for h in range(heads)]
        vcols = [slice(h * DN_DV, (h + 1) * DN_DV) for h in range(heads)]
        ss = [s_scr[h] for h in range(heads)]
        sbs = [s.astype(BF16) for s in ss]
        wq = [jnp.dot(jnp.concatenate([w_ref[0, rows, kcols[h]], qd_ref[0, rows, kcols[h]]], axis=0), sbs[h],
                      preferred_element_type=F32) for h in range(heads)]
        vbs = [(u_ref[0, rows, vcols[h]] - wq[h][:chunk]).astype(BF16) for h in range(heads)]
        for h in range(heads):
            o_ref[0, rows, vcols[h]] = (wq[h][chunk:] + jnp.dot(intra_ref[0, h, rows, :], vbs[h],
                                                                preferred_element_type=F32)).astype(o_ref.dtype)
        for h in range(heads):
            s_scr[h] = (ss[h] * dec_ref[0, h, 0, pl.ds(n, 1), :]
                        + lax.dot_general(kd_ref[0, rows, kcols[h]], vbs[h], _TN, preferred_element_type=F32))
        return carry

    lax.fori_loop(0, nchunk, body, 0)


def _dn_scan(u, w, qd, kd, intra, dec, *, tc):
    b, t, vw = u.shape
    heads = vw // DN_DV
    qw = heads * DN_DK
    nt = t // tc
    nchunk = tc // DN_CHUNK
    assert dec.shape == (b, heads, nt, nchunk, DN_DV)
    return pl.pallas_call(
        functools.partial(_dn_scan_kernel, heads=heads, chunk=DN_CHUNK),
        grid=(b, nt),
        in_specs=[pl.BlockSpec((1, tc, vw), lambda bi, ti: (bi, ti, 0)),
                  pl.BlockSpec((1, tc, qw), lambda bi, ti: (bi, ti, 0)),
                  pl.BlockSpec((1, tc, qw), lambda bi, ti: (bi, ti, 0)),
                  pl.BlockSpec((1, tc, qw), lambda bi, ti: (bi, ti, 0)),
                  pl.BlockSpec((1, heads, tc, DN_CHUNK), lambda bi, ti: (bi, 0, ti, 0)),
                  pl.BlockSpec((1, heads, 1, nchunk, DN_DV), lambda bi, ti: (bi, 0, ti, 0, 0))],
        out_specs=pl.BlockSpec((1, tc, vw), lambda bi, ti: (bi, ti, 0)),
        out_shape=jax.ShapeDtypeStruct((b, t, vw), BF16),
        scratch_shapes=[pltpu.VMEM((heads, DN_DK, DN_DV), F32)],
        compiler_params=_cparams("arbitrary", "arbitrary"),
        name="dn_scan",
    )(u, w, qd, kd, intra, dec)


def _dn_out_kernel(o_ref, gate_ref, gn_ref, w_ref, x_ref, out_ref, *, heads):
    parts = []
    for h in range(heads):
        cols = slice(h * DN_DV, (h + 1) * DN_DV)
        o = o_ref[:, cols].astype(F32)
        ms = jnp.mean(o * o, axis=-1, keepdims=True)
        parts.append((o * lax.rsqrt(ms + RMS_EPS) * gn_ref[...] * gate_ref[:, cols].astype(F32)).astype(BF16))
    out_ref[...] = x_ref[...] + jnp.dot(jnp.concatenate(parts, axis=1), w_ref[...], preferred_element_type=F32)


def _dn_out(o2d, gate, gn, w_out, x2d, *, tm):
    m, vw = o2d.shape
    d = x2d.shape[1]
    heads = vw // DN_DV
    return pl.pallas_call(
        functools.partial(_dn_out_kernel, heads=heads),
        grid=(m // tm,),
        in_specs=[pl.BlockSpec((tm, vw), lambda i: (i, 0)),
                  pl.BlockSpec((tm, vw), lambda i: (i, 0)),
                  pl.BlockSpec((1, DN_DV), lambda i: (0, 0)),
                  pl.BlockSpec((vw, d), lambda i: (0, 0)),
                  pl.BlockSpec((tm, d), lambda i: (i, 0))],
        out_specs=pl.BlockSpec((tm, d), lambda i: (i, 0)),
        out_shape=jax.ShapeDtypeStruct((m, d), F32),
        compiler_params=_cparams("arbitrary"),
        name="dn_out",
    )(o2d, gate, gn.reshape(1, DN_DV), w_out, x2d)


def _deltanet_layer(x, g, w_in, conv_w, a_log, dt_bias, o_norm_g, w_out):
    b, t, d = x.shape
    m = b * t
    x2d = x.reshape(m, d)
    heads = a_log.shape[0]
    qkw = heads * DN_DK
    vw = heads * DN_DV
    tm = min(512, t)
    nq, nv = qkw // IN_CHUNK, vw // IN_CHUNK
    plan = ([("conv_l2", 0, c * IN_CHUNK, DN_DK, DN_DK ** -0.5) for c in range(nq)]
            + [("conv_l2", 1, c * IN_CHUNK, DN_DK, 1.0) for c in range(nq)]
            + [("conv", 2, c * IN_CHUNK, 0, 1.0) for c in range(nv)]
            + [("silu", 3, c * IN_CHUNK, 0, 1.0) for c in range(nv)])
    q, k, v, gate = _in_proj(x2d, g, w_in[:, :2 * qkw + 2 * vw].astype(BF16), plan, (qkw, qkw, vw, vw), tm=tm,
                             seq=t, conv_w=conv_w)
    ab_c, ab_r = _dn_ab(x2d, g, w_in[:, 2 * qkw + 2 * vw:], a_log, dt_bias, tm=tm)
    tc = min(512, t)
    u, w, qd, kd, intra, dec = _dn_prep(q.reshape(b, t, qkw), k.reshape(b, t, qkw), v.reshape(b, t, vw),
                                        ab_c.reshape(b, t, 2 * heads), ab_r, tc=min(DN_PREP_ROWS, t),
                                        dec_group=tc // DN_CHUNK, passes=DN_PREP_PASSES)
    o = _dn_scan(u, w, qd, kd, intra, dec, tc=tc)
    out = _dn_out(o.reshape(m, vw), gate, o_norm_g, w_out.astype(BF16), x2d, tm=tm)
    return out.reshape(b, t, d)


def _sb_kernel(q_ref, k_ref, v_ref, gate_ref, o_ref, zn_s, l1mb_s, wts_s, sum_s, run_s, acc_s, *, dh):
    i = pl.program_id(2)
    tq = q_ref.shape[1]
    tk = tq
    nh = LANES // dh
    nlb = q_ref.shape[2] // LANES
    lane = lax.broadcasted_iota(jnp.int32, (tq, LANES), 1)
    ri = lax.broadcasted_iota(jnp.int32, (tk, tk), 0)
    ci = lax.broadcasted_iota(jnp.int32, (tk, tk), 1)
    later = jnp.where(ri >= ci, 1.0, 0.0).astype(BF16)
    mask = ci < ri

    chains = [(lb, hh) for lb in range(nlb) for hh in range(nh)]
    nc = len(chains)
    qms = []
    for lb, hh in chains:
        q = q_ref[0, :, lb * LANES:(lb + 1) * LANES]
        qms.append(jnp.where((lane >= hh * dh) & (lane < (hh + 1) * dh), q, jnp.zeros_like(q)))

    def kv_block(ref, s):
        start = pl.multiple_of(jnp.maximum(i - s, 0) * tk, tk)
        return [ref[0, pl.ds(start, tk), lb * LANES:(lb + 1) * LANES] for lb in range(nlb)]

    def stage_a(s, slot):
        kjs = kv_block(k_ref, s)
        for c in range(nc):
            zn_s[slot, c] = lax.dot_general(qms[c], kjs[chains[c][0]], _NT, preferred_element_type=F32)

    def stage_b(slot, diag):
        for c in range(nc):
            zn = zn_s[slot, c]
            neg_abs = lax.bitcast_convert_type(lax.bitcast_convert_type(zn, jnp.uint32) | jnp.uint32(SIGN_BIT), F32)
            l1m = jnp.minimum(zn, 0.0) - jnp.log(1.0 + jnp.exp(neg_abs))
            if diag:
                l1m = jnp.where(mask, l1m, 0.0)
            l1mb_s[slot, c] = l1m.astype(BF16)
            sum_s[slot, c] = jnp.broadcast_to(jnp.sum(l1m, axis=-1, keepdims=True), (tq, LANES))

    def stage_c(slot):
        return [jnp.dot(l1mb_s[slot, c], later, preferred_element_type=F32) for c in range(nc)]

    def stage_d(cums, slot, diag):
        for c in range(nc):
            run = run_s[c]
            w = jnp.exp(cums[c] - zn_s[slot, c] + jnp.concatenate([run] * (tk // LANES), axis=1))
            if diag:
                w = jnp.where(mask, w, 0.0)
            wts_s[slot, c] = w.astype(BF16)
            run_s[c] = run + sum_s[slot, c]

    def stage_e(s, slot):
        vjs = kv_block(v_ref, s)
        for c in range(nc):
            acc_s[c] += jnp.dot(wts_s[slot, c], vjs[chains[c][0]], preferred_element_type=F32)

    run_s[...] = jnp.zeros(run_s.shape, F32)
    acc_s[...] = jnp.zeros(acc_s.shape, F32)
    stage_a(0, 0)
    stage_a(1, 1)
    stage_b(0, True)
    cum0 = stage_c(0)
    stage_b(1, False)
    stage_d(cum0, 0, True)
    stage_a(2, 0)

    def trip(t, par):
        cums = stage_c(par)
        stage_e(t - 1, 1 - par)
        stage_b(1 - par, False)
        stage_d(cums, par, False)
        stage_a(t + 2, par)

    def pair(p, carry):
        trip(2 * p + 1, 1)
        trip(2 * p + 2, 0)
        return carry

    lax.fori_loop(0, i // 2, pair, 0)

    @pl.when(i % 2 == 1)
    def _():
        trip(i, 1)

    @pl.when(i % 2 == 1)
    def _():
        stage_e(i, 1)

    @pl.when(i % 2 == 0)
    def _():
        stage_e(i, 0)

    for c in range(0, nc, nh):
        lb = chains[c][0]
        o = acc_s[c]
        for hh in range(1, nh):
            o = jnp.where(lane >= hh * dh, acc_s[c + hh], o)
        cs = slice(lb * LANES, (lb + 1) * LANES)
        o_ref[0, :, cs] = (o * gate_ref[0, :, cs].astype(F32)).astype(o_ref.dtype)


def _sb_attention(q, k, v, gate, *, tq, bw):
    b, t, w = q.shape
    nc = bw // SB_DH
    blk = pl.BlockSpec((1, tq, bw), lambda bi, hi, ti: (bi, ti, hi))
    full = pl.BlockSpec((1, t, bw), lambda bi, hi, ti: (bi, 0, hi))
    return pl.pallas_call(
        functools.partial(_sb_kernel, dh=SB_DH),
        grid=(b, w // bw, t // tq),
        in_specs=[blk, full, full, blk],
        out_specs=blk,
        out_shape=jax.ShapeDtypeStruct((b, t, w), BF16),
        scratch_shapes=[pltpu.VMEM((2, nc, tq, tq), F32),
                        pltpu.VMEM((2, nc, tq, tq), BF16),
                        pltpu.VMEM((2, nc, tq, tq), BF16),
                        pltpu.VMEM((2, nc, tq, LANES), F32),
                        pltpu.VMEM((nc, tq, LANES), F32),
                        pltpu.VMEM((nc, tq, LANES), F32)],
        compiler_params=_cparams("arbitrary", "arbitrary", "arbitrary"),
        name="sb_attn",
    )(q, k, v, gate)


def _stickbreak_layer(x, g, w_in, q_norm_g, k_norm_g, w_out):
    b, t, d = x.shape
    m = b * t
    x2d = x.reshape(m, d)
    w = w_in.shape[1] // 4
    heads = w // SB_DH
    tm = min(512, t)
    gains = jnp.concatenate([jnp.tile(q_norm_g, heads), jnp.tile(k_norm_g, heads)]).reshape(1, 2 * w)
    nw = w // IN_CHUNK
    plan = ([("rms", 0, c * IN_CHUNK, SB_DH, -(SB_DH ** -0.5)) for c in range(nw)]
            + [("rms", 1, c * IN_CHUNK, SB_DH, 1.0) for c in range(nw)]
            + [("plain", 2, c * IN_CHUNK, 0, 1.0) for c in range(nw)]
            + [("silu", 3, c * IN_CHUNK, 0, 1.0) for c in range(nw)])
    q, k, v, gate = _in_proj(x2d, g, w_in.astype(BF16), plan, (w, w, w, w), tm=tm, seq=t, gain=gains)
    o = _sb_attention(q.reshape(b, t, w), k.reshape(b, t, w), v.reshape(b, t, w), gate.reshape(b, t, w),
                      tq=min(256, t), bw=512)
    out = _out_proj(o.reshape(m, w), w_out.astype(BF16), x2d, tm=tm)
    return out.reshape(b, t, d)


def _sc_kernel(x_ref, g_ref, wb_ref, wc_ref, wu_ref, wg_ref, cw_ref, wo_ref, o_ref,
               h_scr, acc_scr, tail_scr, work_scr, *, seq):
    i = pl.program_id(0)
    j = pl.program_id(1)
    tm = x_ref.shape[0]

    @pl.when(j == 0)
    def _():
        _normed_rows(x_ref, g_ref, h_scr)
        acc_scr[...] = x_ref[...]

    h = h_scr[...]
    cu = (jnp.dot(h, wc_ref[...], preferred_element_type=F32) * jnp.dot(h, wu_ref[...], preferred_element_type=F32))

    @pl.when(((i * tm) % seq == 0) & (j == 0))
    def _():
        tail_scr[...] = jnp.zeros(tail_scr.shape, F32)

    y = _causal_conv(cu, cw_ref[...], tail_scr.at[j], work_scr)
    y = y * jnp.dot(h, wb_ref[...], preferred_element_type=F32)
    y = y * _silu(jnp.dot(h, wg_ref[...], preferred_element_type=F32))
    acc_scr[...] += jnp.dot(y.astype(BF16), wo_ref[...], preferred_element_type=F32)

    @pl.when(j == pl.num_programs(1) - 1)
    def _():
        o_ref[...] = acc_scr[...]


def _shortconv_layer(x, g, w_in, conv_w, w_out):
    b, t, d = x.shape
    m = b * t
    x2d = x.reshape(m, d)
    w = w_in.shape[1] // 4
    tm = min(512, t)
    tn = min(1024, w)
    nj = w // tn
    w_bf = w_in.astype(BF16)
    taps = conv_w.shape[0]

    def wspec(part):
        return pl.BlockSpec((d, tn), lambda i, j: (0, part * nj + j))

    out = pl.pallas_call(
        functools.partial(_sc_kernel, seq=t),
        grid=(m // tm, nj),
        in_specs=[pl.BlockSpec((tm, d), lambda i, j: (i, 0)),
                  pl.BlockSpec((1, d), lambda i, j: (0, 0)),
                  wspec(0), wspec(1), wspec(2), wspec(3),
                  pl.BlockSpec((taps, tn), lambda i, j: (0, j)),
                  pl.BlockSpec((tn, d), lambda i, j: (j, 0))],
        out_specs=pl.BlockSpec((tm, d), lambda i, j: (i, 0)),
        out_shape=jax.ShapeDtypeStruct((m, d), F32),
        scratch_shapes=[pltpu.VMEM((tm, d), BF16), pltpu.VMEM((tm, d), F32),
                        pltpu.VMEM((nj, SUBLANES, tn), F32), pltpu.VMEM((SUBLANES + tm, tn), F32)],
        compiler_params=_cparams("arbitrary", "arbitrary"),
        name="shortconv_layer",
    )(x2d, g.reshape(1, d), w_bf, w_bf, w_bf, w_bf, conv_w, w_out.astype(BF16))
    return out.reshape(b, t, d)


def kernel(x, norm_g, dn_w_in, dn_conv_w, dn_a_log, dn_dt_bias, dn_o_norm_g, dn_w_out, sb_w_in, sb_q_norm_g,
           sb_k_norm_g, sb_w_out, sc_w_in, sc_conv_w, sc_w_out):
    depth = norm_g.shape[0]
    n_mixers = 3
    for i in range(depth):
        j = i // n_mixers
        kind = i % n_mixers
        if kind == 0:
            x = _deltanet_layer(x, norm_g[i], dn_w_in[j], dn_conv_w[j], dn_a_log[j], dn_dt_bias[j],
                                dn_o_norm_g[j], dn_w_out[j])
        elif kind == 1:
            x = _stickbreak_layer(x, norm_g[i], sb_w_in[j], sb_q_norm_g[j], sb_k_norm_g[j], sb_w_out[j])
        else:
            x = _shortconv_layer(x, norm_g[i], sc_w_in[j], sc_conv_w[j], sc_w_out[j])
    return x
```

```python
import functools

import jax
import jax.numpy as jnp
from jax import lax
from jax.experimental import pallas as pl
from jax.experimental.pallas import tpu as pltpu

F32 = jnp.float32
BF16 = jnp.bfloat16
HIGHEST = lax.Precision.HIGHEST

SIGN_BIT = 0x80000000
RMS_EPS = 1e-6
L2_EPS = 1e-6
DN_HEADS = 8
DN_DK = 128
DN_DV = 256
DN_CHUNK = 64
DN_PREP_ROWS = 2048
DN_PREP_PASSES = 1
SB_DH = 64
LANES = 128
SUBLANES = 8
VMEM_LIMIT = 56 * 1024 * 1024

_NT = (((1,), (1,)), ((), ()))
_TN = (((0,), (0,)), ((), ()))


def _cparams(*sem):
    return pltpu.CompilerParams(dimension_semantics=sem, vmem_limit_bytes=VMEM_LIMIT)


def _silu(y):
    half = 0.5 * y
    return half + half * jnp.tanh(half)


def _softplus(y):
    return jnp.maximum(y, 0.0) + jnp.log1p(jnp.exp(-jnp.abs(y)))


def _normed_rows(x_ref, g_ref, h_scr):
    x = x_ref[...]
    ms = jnp.mean(x * x, axis=-1, keepdims=True)
    h_scr[...] = (x * lax.rsqrt(ms + RMS_EPS) * g_ref[...]).astype(h_scr.dtype)


def _causal_conv(acc, cw, tail_ref, work_ref):
    tm = acc.shape[0]
    taps = cw.shape[0]
    work_ref[0:SUBLANES, :] = tail_ref[...]
    work_ref[SUBLANES:SUBLANES + tm, :] = acc
    tail_ref[...] = acc[tm - SUBLANES:tm, :]
    y = acc * cw[taps - 1:taps, :]
    for s in range(1, taps):
        y = y + work_ref[SUBLANES - s:SUBLANES - s + tm, :] * cw[taps - 1 - s:taps - s, :]
    return y


def _group_sum_sq(y, group):
    width = 2 * LANES
    r = lax.broadcasted_iota(jnp.int32, (width, width), 0) // group
    c = lax.broadcasted_iota(jnp.int32, (width, width), 1) // group
    ones = jnp.where(r == c, 1.0, 0.0).astype(BF16)
    sq = (y * y).astype(BF16)
    parts = [jnp.dot(sq[:, s:s + width], ones, preferred_element_type=F32) for s in range(0, y.shape[1], width)]
    return jnp.concatenate(parts, axis=-1)


IN_CHUNK = 512


def _in_proj_kernel(*refs, plan, seq, n_out, n_conv, has_gain):
    refs = list(refs)
    x_ref, g_ref, w_ref = refs[:3]
    pos = 3
    cw_ref = gn_ref = tail_scr = work_scr = None
    if n_conv:
        cw_ref = refs[pos]
        pos += 1
    if has_gain:
        gn_ref = refs[pos]
        pos += 1
    out_refs = refs[pos:pos + n_out]
    if n_conv:
        tail_scr, work_scr = refs[pos + n_out:pos + n_out + 2]
    tm = x_ref.shape[0]
    x = x_ref[...]
    ms = jnp.mean(x * x, axis=-1, keepdims=True)
    h = (x * lax.rsqrt(ms + RMS_EPS) * g_ref[...]).astype(BF16)

    if n_conv:
        @pl.when((pl.program_id(0) * tm) % seq == 0)
        def _():
            tail_scr[...] = jnp.zeros(tail_scr.shape, F32)

    for c, (mode, out, off, group, scale) in enumerate(plan):
        cols = slice(c * IN_CHUNK, (c + 1) * IN_CHUNK)
        acc = jnp.dot(h, w_ref[:, cols], preferred_element_type=F32)
        if mode == "plain":
            y = acc
        elif mode == "silu":
            y = _silu(acc)
        elif mode in ("conv", "conv_l2"):
            y = _silu(_causal_conv(acc, cw_ref[:, cols], tail_scr.at[c], work_scr.at[c]))
            if mode == "conv_l2":
                y = y * (lax.rsqrt(_group_sum_sq(y, group) + L2_EPS) * scale)
        elif mode == "rms":
            y = acc * lax.rsqrt(_group_sum_sq(acc, group) * (1.0 / group) + RMS_EPS) * (gn_ref[:, cols] * scale)
        out_refs[out][:, off:off + IN_CHUNK] = y.astype(out_refs[out].dtype)


def _in_proj(x2d, g, w, plan, out_widths, *, tm, seq, conv_w=None, gain=None):
    m, d = x2d.shape
    n = w.shape[1]
    assert m % tm == 0 and seq % tm == 0 and n == len(plan) * IN_CHUNK
    n_conv = sum(1 for p in plan if p[0].startswith("conv"))
    assert all(p[0].startswith("conv") for p in plan[:n_conv])
    const = lambda i: (0, 0)
    in_specs = [pl.BlockSpec((tm, d), lambda i: (i, 0)), pl.BlockSpec((1, d), const),
                pl.BlockSpec((d, n), const, pipeline_mode=pl.Buffered(1))]
    args = [x2d, g.reshape(1, d), w]
    scratch = []
    if n_conv:
        in_specs.append(pl.BlockSpec(conv_w.shape, const))
        args.append(conv_w)
        scratch = [pltpu.VMEM((n_conv, SUBLANES, IN_CHUNK), F32), pltpu.VMEM((n_conv, SUBLANES + tm, IN_CHUNK), F32)]
    if gain is not None:
        in_specs.append(pl.BlockSpec(gain.shape, const))
        args.append(gain)
    return pl.pallas_call(
        functools.partial(_in_proj_kernel, plan=tuple(plan), seq=seq, n_out=len(out_widths), n_conv=n_conv,
                          has_gain=gain is not None),
        grid=(m // tm,),
        in_specs=in_specs,
        out_specs=[pl.BlockSpec((tm, ow), lambda i: (i, 0)) for ow in out_widths],
        out_shape=[jax.ShapeDtypeStruct((m, ow), BF16) for ow in out_widths],
        scratch_shapes=scratch,
        compiler_params=_cparams("arbitrary"),
        name="in_proj",
    )(*args)


def _out_kernel(a_ref, w_ref, x_ref, o_ref):
    o_ref[...] = x_ref[...] + jnp.dot(a_ref[...], w_ref[...], preferred_element_type=F32)


def _out_proj(a, w, x2d, *, tm):
    m, k = a.shape
    d = w.shape[1]
    return pl.pallas_call(
        _out_kernel,
        grid=(m // tm,),
        in_specs=[pl.BlockSpec((tm, k), lambda i: (i, 0)),
                  pl.BlockSpec((k, d), lambda i: (0, 0)),
                  pl.BlockSpec((tm, d), lambda i: (i, 0))],
        out_specs=pl.BlockSpec((tm, d), lambda i: (i, 0)),
        out_shape=jax.ShapeDtypeStruct((m, d), F32),
        compiler_params=_cparams("arbitrary"),
        name="out_proj",
    )(a, w, x2d)


def _ab_kernel(x_ref, g_ref, w3_ref, p_ref, oc_ref, or_ref, *, heads, chunk):
    x = x_ref[...]
    tm = x.shape[0]
    ms = jnp.mean(x * x, axis=-1, keepdims=True)
    hh, hl = _split(x * lax.rsqrt(ms + RMS_EPS) * g_ref[...])
    acc = jnp.dot(jnp.concatenate([hh, hl, hh], axis=1), w3_ref[...], preferred_element_type=F32)
    la = -jnp.exp(p_ref[0:1, :]) * _softplus(acc + p_ref[1:2, :])
    beta = jax.nn.sigmoid(acc)
    r = lax.broadcasted_iota(jnp.int32, (tm, tm), 0)
    c = lax.broadcasted_iota(jnp.int32, (tm, tm), 1)
    tril = jnp.where(((r // chunk) == (c // chunk)) & (c <= r), 1.0, 0.0).astype(BF16)
    lah, lal = _split(la)
    g_cum = jnp.dot(jnp.concatenate([tril, tril], axis=1), jnp.concatenate([lah, lal], axis=0),
                    preferred_element_type=F32)
    out = jnp.where(lax.broadcasted_iota(jnp.int32, acc.shape, 1) < heads, g_cum, beta)
    oc_ref[...] = out[:, :2 * heads]
    or_ref[...] = out.T[:2 * heads, :]


def _dn_ab(x2d, g, w_ab, a_log, dt_bias, *, tm):
    m, d = x2d.shape
    heads = a_log.shape[0]
    pad = LANES - 2 * heads
    w_pad = jnp.pad(w_ab, ((0, 0), (0, pad)))
    w_hi = w_pad.astype(BF16)
    w_lo = (w_pad - w_hi.astype(F32)).astype(BF16)
    w3 = jnp.concatenate([w_hi, w_hi, w_lo], axis=0)
    params = jnp.stack([jnp.pad(a_log, (0, LANES - heads)), jnp.pad(dt_bias, (0, LANES - heads))])
    return pl.pallas_call(
        functools.partial(_ab_kernel, heads=heads, chunk=DN_CHUNK),
        grid=(m // tm,),
        in_specs=[pl.BlockSpec((tm, d), lambda i: (i, 0)),
                  pl.BlockSpec((1, d), lambda i: (0, 0)),
                  pl.BlockSpec((3 * d, LANES), lambda i: (0, 0)),
                  pl.BlockSpec((2, LANES), lambda i: (0, 0))],
        out_specs=[pl.BlockSpec((tm, 2 * heads), lambda i: (i, 0)),
                   pl.BlockSpec((2 * heads, tm), lambda i: (0, i))],
        out_shape=[jax.ShapeDtypeStruct((m, 2 * heads), F32), jax.ShapeDtypeStruct((2 * heads, m), F32)],
        compiler_params=_cparams("arbitrary"),
        name="dn_ab",
    )(x2d, g.reshape(1, d), w3, params)


def _split(a):
    hi = a.astype(BF16)
    return hi, (a - hi.astype(F32)).astype(BF16)


def _mm(a, b, dims, passes):
    dg = functools.partial(lax.dot_general, dimension_numbers=dims, preferred_element_type=F32)
    if passes == 6:
        return dg(a, b, precision=HIGHEST)
    if passes == 1:
        return dg(a.astype(BF16), b.astype(BF16))
    ah, al = _split(a)
    bh, bl = _split(b)
    return dg(ah, bh) + (dg(ah, bl) + dg(al, bh))


_NN = (((1,), (0,)), ((), ()))


def _dn_prep_kernel(q_ref, k_ref, v_ref, gbc_ref, gbr_ref,
                    u_ref, w_ref, qd_ref, kd_ref, intra_ref, dec_ref, *, heads, chunk, passes):
    h = pl.program_id(1)
    tc = q_ref.shape[1]
    units = range(tc // chunk)
    gbc = gbc_ref[0]
    lane = lax.broadcasted_iota(jnp.int32, gbc.shape, 1)
    g_col = jnp.sum(jnp.where(lane == h, gbc, 0.0), axis=-1, keepdims=True)
    beta_col = jnp.sum(jnp.where(lane == h + heads, gbc, 0.0), axis=-1, keepdims=True)
    g_row = gbr_ref[pl.ds(h, 1), :]

    ri = lax.broadcasted_iota(jnp.int32, (chunk, chunk), 0)
    ci = lax.broadcasted_iota(jnp.int32, (chunk, chunk), 1)
    causal = ci <= ri
    strict = ci < ri
    eye = jnp.where(ci == ri, 1.0, 0.0).astype(F32)
    rows = [slice(n * chunk, (n + 1) * chunk) for n in units]

    ks = [k_ref[0, rs, :] for rs in rows]
    qs = [q_ref[0, rs, :] for rs in rows]
    gcs = [g_col[rs] for rs in rows]
    bcs = [beta_col[rs] for rs in rows]
    kbs = [ks[n] * bcs[n] for n in units]
    egs = [jnp.exp(gcs[n]) for n in units]
    decays = [jnp.where(causal, jnp.exp(jnp.where(causal, gcs[n] - g_row[:, rows[n]], 0.0)), 0.0) for n in units]
    kks = [_mm(kbs[n], ks[n], _NT, passes) for n in units]
    qks = [_mm(qs[n], ks[n], _NT, 1) for n in units]
    for n in units:
        intra_ref[0, 0, rows[n], :] = jnp.where(causal, qks[n] * decays[n], 0.0).astype(intra_ref.dtype)
        qd_ref[0, rows[n], :] = (qs[n] * egs[n]).astype(qd_ref.dtype)
        g_last = gcs[n][chunk - 1:chunk, :]
        kd_ref[0, rows[n], :] = (ks[n] * jnp.exp(g_last - gcs[n])).astype(kd_ref.dtype)
        grp = dec_ref.shape[3]
        dec_ref[0, 0, n // grp, n % grp:n % grp + 1, :] = jnp.broadcast_to(jnp.exp(g_last), (1, dec_ref.shape[-1]))

    ls = [jnp.where(strict, kks[n] * decays[n], 0.0) for n in units]

    def quarter(b):
        return ((ri // (2 * b)) == (ci // (2 * b))) & ((ri % (2 * b)) >= b) & ((ci % (2 * b)) < b)

    tinvs = [eye - jnp.where(quarter(1), ls[n], 0.0) for n in units]
    b = 2
    while b < chunk:
        cs = [jnp.where(quarter(b), ls[n], 0.0) for n in units]
        tcs = [_mm(tinvs[n], cs[n], _NN, passes) for n in units]
        tinvs = [tinvs[n] - _mm(tcs[n], tinvs[n], _NN, passes) for n in units]
        b *= 2
    for n in units:
        u_ref[0, rows[n], :] = _mm(tinvs[n], v_ref[0, rows[n], :] * bcs[n], _NN, passes).astype(u_ref.dtype)
    for n in units:
        w_ref[0, rows[n], :] = _mm(tinvs[n], kbs[n] * egs[n], _NN, passes).astype(w_ref.dtype)


def _dn_prep(q, k, v, ab_c, ab_r, *, tc, dec_group, passes):
    b, t, qw = q.shape
    heads = qw // DN_DK
    nt = t // tc
    grp = dec_group
    ngrp = tc // (DN_CHUNK * grp)
    qk_spec = pl.BlockSpec((1, tc, DN_DK), lambda bi, hi, ti: (bi, ti, hi))
    v_spec = pl.BlockSpec((1, tc, DN_DV), lambda bi, hi, ti: (bi, ti, hi))
    return pl.pallas_call(
        functools.partial(_dn_prep_kernel, heads=heads, chunk=DN_CHUNK, passes=passes),
        grid=(b, heads, nt),
        in_specs=[qk_spec, qk_spec, v_spec,
                  pl.BlockSpec((1, tc, 2 * heads), lambda bi, hi, ti: (bi, ti, 0)),
                  pl.BlockSpec((2 * heads, tc), lambda bi, hi, ti: (0, bi * nt + ti))],
        out_specs=[v_spec, qk_spec, qk_spec, qk_spec,
                   pl.BlockSpec((1, 1, tc, DN_CHUNK), lambda bi, hi, ti: (bi, hi, ti, 0)),
                   pl.BlockSpec((1, 1, ngrp, grp, DN_DV), lambda bi, hi, ti: (bi, hi, ti, 0, 0))],
        out_shape=[jax.ShapeDtypeStruct((b, t, heads * DN_DV), BF16),
                   jax.ShapeDtypeStruct((b, t, qw), BF16),
                   jax.ShapeDtypeStruct((b, t, qw), BF16),
                   jax.ShapeDtypeStruct((b, t, qw), BF16),
                   jax.ShapeDtypeStruct((b, heads, t, DN_CHUNK), BF16),
                   jax.ShapeDtypeStruct((b, heads, nt * ngrp, grp, DN_DV), F32)],
        compiler_params=_cparams("arbitrary", "arbitrary", "arbitrary"),
        name="dn_prep",
    )(q, k, v, ab_c, ab_r)


def _dn_scan_kernel(u_ref, w_ref, qd_ref, kd_ref, intra_ref, dec_ref, o_ref, s_scr, *, heads, chunk):
    nchunk = u_ref.shape[1] // chunk

    @pl.when(pl.program_id(1) == 0)
    def _():
        s_scr[...] = jnp.zeros(s_scr.shape, F32)

    def body(n, carry):
        rows = pl.ds(pl.multiple_of(n * chunk, chunk), chunk)
        kcols = [slice(h * DN_DK, (h + 1) * DN_DK) for h in range(heads)]
        vcols = [slice(h * DN_DV, (h + 1) * DN_DV) for h in range(heads)]
        ss = [s_scr[h] for h in range(heads)]
        sbs = [s.astype(BF16) for s in ss]
        wq = [jnp.dot(jnp.concatenate([w_ref[0, rows, kcols[h]], qd_ref[0, rows, kcols[h]]], axis=0), sbs[h],
                      preferred_element_type=F32) for h in range(heads)]
        vbs = [(u_ref[0, rows, vcols[h]] - wq[h][:chunk]).astype(BF16) for h in range(heads)]
        for h in range(heads):
            o_ref[0, rows, vcols[h]] = (wq[h][chunk:] + jnp.dot(intra_ref[0, h, rows, :], vbs[h],
                                                                preferred_element_type=F32)).astype(o_ref.dtype)
        for h in range(heads):
            s_scr[h] = (ss[h] * dec_ref[0, h, 0, pl.ds(n, 1), :]
                        + lax.dot_general(kd_ref[0, rows, kcols[h]], vbs[h], _TN, preferred_element_type=F32))
        return carry

    lax.fori_loop(0, nchunk, body, 0)


def _dn_scan(u, w, qd, kd, intra, dec, *, tc):
    b, t, vw = u.shape
    heads = vw // DN_DV
    qw = heads * DN_DK
    nt = t // tc
    nchunk = tc // DN_CHUNK
    assert dec.shape == (b, heads, nt, nchunk, DN_DV)
    return pl.pallas_call(
        functools.partial(_dn_scan_kernel, heads=heads, chunk=DN_CHUNK),
        grid=(b, nt),
        in_specs=[pl.BlockSpec((1, tc, vw), lambda bi, ti: (bi, ti, 0)),
                  pl.BlockSpec((1, tc, qw), lambda bi, ti: (bi, ti, 0)),
                  pl.BlockSpec((1, tc, qw), lambda bi, ti: (bi, ti, 0)),
                  pl.BlockSpec((1, tc, qw), lambda bi, ti: (bi, ti, 0)),
                  pl.BlockSpec((1, heads, tc, DN_CHUNK), lambda bi, ti: (bi, 0, ti, 0)),
                  pl.BlockSpec((1, heads, 1, nchunk, DN_DV), lambda bi, ti: (bi, 0, ti, 0, 0))],
        out_specs=pl.BlockSpec((1, tc, vw), lambda bi, ti: (bi, ti, 0)),
        out_shape=jax.ShapeDtypeStruct((b, t, vw), BF16),
        scratch_shapes=[pltpu.VMEM((heads, DN_DK, DN_DV), F32)],
        compiler_params=_cparams("arbitrary", "arbitrary"),
        name="dn_scan",
    )(u, w, qd, kd, intra, dec)


def _dn_out_kernel(o_ref, gate_ref, gn_ref, w_ref, x_ref, out_ref, *, heads):
    parts = []
    for h in range(heads):
        cols = slice(h * DN_DV, (h + 1) * DN_DV)
        o = o_ref[:, cols].astype(F32)
        ms = jnp.mean(o * o, axis=-1, keepdims=True)
        parts.append((o * lax.rsqrt(ms + RMS_EPS) * gn_ref[...] * gate_ref[:, cols].astype(F32)).astype(BF16))
    out_ref[...] = x_ref[...] + jnp.dot(jnp.concatenate(parts, axis=1), w_ref[...], preferred_element_type=F32)


def _dn_out(o2d, gate, gn, w_out, x2d, *, tm):
    m, vw = o2d.shape
    d = x2d.shape[1]
    heads = vw // DN_DV
    return pl.pallas_call(
        functools.partial(_dn_out_kernel, heads=heads),
        grid=(m // tm,),
        in_specs=[pl.BlockSpec((tm, vw), lambda i: (i, 0)),
                  pl.BlockSpec((tm, vw), lambda i: (i, 0)),
                  pl.BlockSpec((1, DN_DV), lambda i: (0, 0)),
                  pl.BlockSpec((vw, d), lambda i: (0, 0), pipeline_mode=pl.Buffered(1)),
                  pl.BlockSpec((tm, d), lambda i: (i, 0))],
        out_specs=pl.BlockSpec((tm, d), lambda i: (i, 0)),
        out_shape=jax.ShapeDtypeStruct((m, d), F32),
        compiler_params=_cparams("arbitrary"),
        name="dn_out",
    )(o2d, gate, gn.reshape(1, DN_DV), w_out, x2d)


def _deltanet_layer(x, g, w_in, conv_w, a_log, dt_bias, o_norm_g, w_out):
    b, t, d = x.shape
    m = b * t
    x2d = x.reshape(m, d)
    heads = a_log.shape[0]
    qkw = heads * DN_DK
    vw = heads * DN_DV
    tm = min(512, t)
    nq, nv = qkw // IN_CHUNK, vw // IN_CHUNK
    plan = ([("conv_l2", 0, c * IN_CHUNK, DN_DK, DN_DK ** -0.5) for c in range(nq)]
            + [("conv_l2", 1, c * IN_CHUNK, DN_DK, 1.0) for c in range(nq)]
            + [("conv", 2, c * IN_CHUNK, 0, 1.0) for c in range(nv)]
            + [("silu", 3, c * IN_CHUNK, 0, 1.0) for c in range(nv)])
    q, k, v, gate = _in_proj(x2d, g, w_in[:, :2 * qkw + 2 * vw].astype(BF16), plan, (qkw, qkw, vw, vw), tm=tm,
                             seq=t, conv_w=conv_w)
    ab_c, ab_r = _dn_ab(x2d, g, w_in[:, 2 * qkw + 2 * vw:], a_log, dt_bias, tm=tm)
    tc = min(512, t)
    u, w, qd, kd, intra, dec = _dn_prep(q.reshape(b, t, qkw), k.reshape(b, t, qkw), v.reshape(b, t, vw),
                                        ab_c.reshape(b, t, 2 * heads), ab_r, tc=min(DN_PREP_ROWS, t),
                                        dec_group=tc // DN_CHUNK, passes=DN_PREP_PASSES)
    o = _dn_scan(u, w, qd, kd, intra, dec, tc=tc)
    out = _dn_out(o.reshape(m, vw), gate, o_norm_g, w_out.astype(BF16), x2d, tm=tm)
    return out.reshape(b, t, d)


def _sb_kernel(q_ref, k_ref, v_ref, gate_ref, o_ref, zn_s, l1mb_s, wts_s, sum_s, run_s, acc_s, *, dh):
    i = pl.program_id(2)
    tq = q_ref.shape[1]
    tk = tq
    nh = LANES // dh
    nlb = q_ref.shape[2] // LANES
    lane = lax.broadcasted_iota(jnp.int32, (tq, LANES), 1)
    ri = lax.broadcasted_iota(jnp.int32, (tk, tk), 0)
    ci = lax.broadcasted_iota(jnp.int32, (tk, tk), 1)
    later = jnp.where(ri >= ci, 1.0, 0.0).astype(BF16)
    mask = ci < ri

    chains = [(lb, hh) for lb in range(nlb) for hh in range(nh)]
    nc = len(chains)
    qms = []
    for lb, hh in chains:
        q = q_ref[0, :, lb * LANES:(lb + 1) * LANES]
        qms.append(jnp.where((lane >= hh * dh) & (lane < (hh + 1) * dh), q, jnp.zeros_like(q)))

    def kv_block(ref, s):
        start = pl.multiple_of(jnp.maximum(i - s, 0) * tk, tk)
        return [ref[0, pl.ds(start, tk), lb * LANES:(lb + 1) * LANES] for lb in range(nlb)]

    def stage_a(s, slot):
        kjs = kv_block(k_ref, s)
        for c in range(nc):
            zn_s[slot, c] = lax.dot_general(qms[c], kjs[chains[c][0]], _NT, preferred_element_type=F32)

    def stage_b(slot, diag):
        for c in range(nc):
            zn = zn_s[slot, c]
            neg_abs = lax.bitcast_convert_type(lax.bitcast_convert_type(zn, jnp.uint32) | jnp.uint32(SIGN_BIT), F32)
            l1m = jnp.minimum(zn, 0.0) - jnp.log(1.0 + jnp.exp(neg_abs))
            if diag:
                l1m = jnp.where(mask, l1m, 0.0)
            l1mb_s[slot, c] = l1m.astype(BF16)
            sum_s[slot, c] = jnp.broadcast_to(jnp.sum(l1m, axis=-1, keepdims=True), (tq, LANES))

    def stage_c(slot):
        return [jnp.dot(l1mb_s[slot, c], later, preferred_element_type=F32) for c in range(nc)]

    def stage_d(cums, slot, diag):
        for c in range(nc):
            run = run_s[c]
            w = jnp.exp(cums[c] - zn_s[slot, c] + jnp.concatenate([run] * (tk // LANES), axis=1))
            if diag:
                w = jnp.where(mask, w, 0.0)
            wts_s[slot, c] = w.astype(BF16)
            run_s[c] = run + sum_s[slot, c]

    def stage_e(s, slot):
        vjs = kv_block(v_ref, s)
        for c in range(nc):
            acc_s[c] += jnp.dot(wts_s[slot, c], vjs[chains[c][0]], preferred_element_type=F32)

    run_s[...] = jnp.zeros(run_s.shape, F32)
    acc_s[...] = jnp.zeros(acc_s.shape, F32)
    stage_a(0, 0)
    stage_a(1, 1)
    stage_b(0, True)
    cum0 = stage_c(0)
    stage_b(1, False)
    stage_d(cum0, 0, True)
    stage_a(2, 0)

    def trip(t, par):
        cums = stage_c(par)
        stage_e(t - 1, 1 - par)
        stage_b(1 - par, False)
        stage_d(cums, par, False)
        stage_a(t + 2, par)

    def pair(p, carry):
        trip(2 * p + 1, 1)
        trip(2 * p + 2, 0)
        return carry

    lax.fori_loop(0, i // 2, pair, 0)

    @pl.when(i % 2 == 1)
    def _():
        trip(i, 1)

    @pl.when(i % 2 == 1)
    def _():
        stage_e(i, 1)

    @pl.when(i % 2 == 0)
    def _():
        stage_e(i, 0)

    for c in range(0, nc, nh):
        lb = chains[c][0]
        o = acc_s[c]
        for hh in range(1, nh):
            o = jnp.where(lane >= hh * dh, acc_s[c + hh], o)
        cs = slice(lb * LANES, (lb + 1) * LANES)
        o_ref[0, :, cs] = (o * gate_ref[0, :, cs].astype(F32)).astype(o_ref.dtype)


def _sb_attention(q, k, v, gate, *, tq, bw):
    b, t, w = q.shape
    nc = bw // SB_DH
    blk = pl.BlockSpec((1, tq, bw), lambda bi, hi, ti: (bi, ti, hi))
    full = pl.BlockSpec((1, t, bw), lambda bi, hi, ti: (bi, 0, hi))
    return pl.pallas_call(
        functools.partial(_sb_kernel, dh=SB_DH),
        grid=(b, w // bw, t // tq),
        in_specs=[blk, full, full, blk],
        out_specs=blk,
        out_shape=jax.ShapeDtypeStruct((b, t, w), BF16),
        scratch_shapes=[pltpu.VMEM((2, nc, tq, tq), F32),
                        pltpu.VMEM((2, nc, tq, tq), BF16),
                        pltpu.VMEM((2, nc, tq, tq), BF16),
                        pltpu.VMEM((2, nc, tq, LANES), F32),
                        pltpu.VMEM((nc, tq, LANES), F32),
                        pltpu.VMEM((nc, tq, LANES), F32)],
        compiler_params=_cparams("arbitrary", "arbitrary", "arbitrary"),
        name="sb_attn",
    )(q, k, v, gate)


def _stickbreak_layer(x, g, w_in, q_norm_g, k_norm_g, w_out):
    b, t, d = x.shape
    m = b * t
    x2d = x.reshape(m, d)
    w = w_in.shape[1] // 4
    heads = w // SB_DH
    tm = min(512, t)
    gains = jnp.concatenate([jnp.tile(q_norm_g, heads), jnp.tile(k_norm_g, heads)]).reshape(1, 2 * w)
    nw = w // IN_CHUNK
    plan = ([("rms", 0, c * IN_CHUNK, SB_DH, -(SB_DH ** -0.5)) for c in range(nw)]
            + [("rms", 1, c * IN_CHUNK, SB_DH, 1.0) for c in range(nw)]
            + [("plain", 2, c * IN_CHUNK, 0, 1.0) for c in range(nw)]
            + [("silu", 3, c * IN_CHUNK, 0, 1.0) for c in range(nw)])
    q, k, v, gate = _in_proj(x2d, g, w_in.astype(BF16), plan, (w, w, w, w), tm=tm, seq=t, gain=gains)
    o = _sb_attention(q.reshape(b, t, w), k.reshape(b, t, w), v.reshape(b, t, w), gate.reshape(b, t, w),
                      tq=min(256, t), bw=512)
    out = _out_proj(o.reshape(m, w), w_out.astype(BF16), x2d, tm=tm)
    return out.reshape(b, t, d)


def _sc_kernel(x_ref, g_ref, wb_ref, wc_ref, wu_ref, wg_ref, cw_ref, wo_ref, o_ref,
               h_scr, acc_scr, tail_scr, work_scr, *, seq):
    i = pl.program_id(0)
    j = pl.program_id(1)
    tm = x_ref.shape[0]

    @pl.when(j == 0)
    def _():
        _normed_rows(x_ref, g_ref, h_scr)
        acc_scr[...] = x_ref[...]

    h = h_scr[...]
    cu = (jnp.dot(h, wc_ref[...], preferred_element_type=F32) * jnp.dot(h, wu_ref[...], preferred_element_type=F32))

    @pl.when(((i * tm) % seq == 0) & (j == 0))
    def _():
        tail_scr[...] = jnp.zeros(tail_scr.shape, F32)

    y = _causal_conv(cu, cw_ref[...], tail_scr.at[j], work_scr)
    y = y * jnp.dot(h, wb_ref[...], preferred_element_type=F32)
    y = y * _silu(jnp.dot(h, wg_ref[...], preferred_element_type=F32))
    acc_scr[...] += jnp.dot(y.astype(BF16), wo_ref[...], preferred_element_type=F32)

    @pl.when(j == pl.num_programs(1) - 1)
    def _():
        o_ref[...] = acc_scr[...]


def _shortconv_layer(x, g, w_in, conv_w, w_out):
    b, t, d = x.shape
    m = b * t
    x2d = x.reshape(m, d)
    w = w_in.shape[1] // 4
    tm = min(512, t)
    tn = min(1024, w)
    nj = w // tn
    w_bf = w_in.astype(BF16)
    taps = conv_w.shape[0]

    def wspec(part):
        return pl.BlockSpec((d, tn), lambda i, j: (0, part * nj + j))

    out = pl.pallas_call(
        functools.partial(_sc_kernel, seq=t),
        grid=(m // tm, nj),
        in_specs=[pl.BlockSpec((tm, d), lambda i, j: (i, 0)),
                  pl.BlockSpec((1, d), lambda i, j: (0, 0)),
                  wspec(0), wspec(1), wspec(2), wspec(3),
                  pl.BlockSpec((taps, tn), lambda i, j: (0, j)),
                  pl.BlockSpec((tn, d), lambda i, j: (j, 0))],
        out_specs=pl.BlockSpec((tm, d), lambda i, j: (i, 0)),
        out_shape=jax.ShapeDtypeStruct((m, d), F32),
        scratch_shapes=[pltpu.VMEM((tm, d), BF16), pltpu.VMEM((tm, d), F32),
                        pltpu.VMEM((nj, SUBLANES, tn), F32), pltpu.VMEM((SUBLANES + tm, tn), F32)],
        compiler_params=_cparams("arbitrary", "arbitrary"),
        name="shortconv_layer",
    )(x2d, g.reshape(1, d), w_bf, w_bf, w_bf, w_bf, conv_w, w_out.astype(BF16))
    return out.reshape(b, t, d)


def kernel(x, norm_g, dn_w_in, dn_conv_w, dn_a_log, dn_dt_bias, dn_o_norm_g, dn_w_out, sb_w_in, sb_q_norm_g,
           sb_k_norm_g, sb_w_out, sc_w_in, sc_conv_w, sc_w_out):
    depth = norm_g.shape[0]
    n_mixers = 3
    for i in range(depth):
        j = i // n_mixers
        kind = i % n_mixers
        if kind == 0:
            x = _deltanet_layer(x, norm_g[i], dn_w_in[j], dn_conv_w[j], dn_a_log[j], dn_dt_bias[j],
                                dn_o_norm_g[j], dn_w_out[j])
        elif kind == 1:
            x = _stickbreak_layer(x, norm_g[i], sb_w_in[j], sb_q_norm_g[j], sb_k_norm_g[j], sb_w_out[j])
        else:
            x = _shortconv_layer(x, norm_g[i], sc_w_in[j], sc_conv_w[j], sc_w_out[j])
    return x
```
